```python
import jax, jax.numpy as jnp
from jax import lax
import numpy as np

D_MODEL = 2048
BATCH = 2
SEQ = 16384
DEPTH = 2

N_A_LAYERS = DEPTH // 2
N_B_LAYERS = DEPTH - N_A_LAYERS
RET_HEADS = 8
RET_KEY_DIM = D_MODEL // RET_HEADS
RET_VAL_DIM = 2 * D_MODEL // RET_HEADS
RET_CHUNK = 128
SB_HEADS = 16
SB_HEAD_DIM = D_MODEL // SB_HEADS
SB_BLOCK = 128
D_FF = 2 * D_MODEL
CONV_WIDTH = 3
LN_EPS = 1e-5
GN_EPS = 1e-6
ROPE_BASE = 10000.0
DEEPNORM_ALPHA = (2.0 * DEPTH) ** 0.25
DEEPNORM_BETA = (8.0 * DEPTH) ** -0.25

kernel_name = "yoco_retention_stickbreaking_convffn_deepnorm"


def layer_norm(x, g, b):
    xf = x.astype(jnp.float32)
    mu = jnp.mean(xf, axis=-1, keepdims=True)
    xc = xf - mu
    var = jnp.mean(xc * xc, axis=-1, keepdims=True)
    y = xc * lax.rsqrt(var + LN_EPS) * g.astype(jnp.float32) + b.astype(jnp.float32)
    return y.astype(x.dtype)


def rotary_tables(seq):
    inv = 1.0 / (ROPE_BASE ** jnp.linspace(0.0, 1.0, RET_KEY_DIM // 2, dtype=jnp.float32))
    ang = jnp.arange(seq, dtype=jnp.float32)[:, None] * inv[None, :]
    return jnp.cos(ang)[:, None, :], jnp.sin(ang)[:, None, :]


def theta_shift(x, cos, sin):
    xs = x.reshape(x.shape[:-1] + (x.shape[-1] // 2, 2))
    x0, x1 = xs[..., 0], xs[..., 1]
    y = jnp.stack([x0 * cos - x1 * sin, x1 * cos + x0 * sin], axis=-1)
    return y.reshape(x.shape)


def chunkwise_retention(q, k, v):
    bsz, seq, nh, dk = q.shape
    dv = v.shape[-1]
    c = RET_CHUNK
    nc = seq // c
    log_gamma = jnp.log1p(-jnp.exp2(-5.0 - jnp.arange(nh, dtype=jnp.float32)))
    idx = jnp.arange(c, dtype=jnp.float32)
    diff = idx[:, None] - idx[None, :]
    decay_in = jnp.where(diff[None] >= 0.0,
                         jnp.exp(jnp.maximum(diff, 0.0)[None] * log_gamma[:, None, None]), 0.0)
    xi = jnp.exp((idx[None, :] + 1.0) * log_gamma[:, None])
    zeta = jnp.exp((c - 1.0 - idx[None, :]) * log_gamma[:, None])
    gamma_c = jnp.exp(c * log_gamma)

    def to_chunks(t):
        return t.reshape(bsz, nc, c, nh, t.shape[-1]).transpose(1, 0, 3, 2, 4)

    qc, kc, vc = to_chunks(q), to_chunks(k), to_chunks(v)

    def step(state, inp):
        qi, ki, vi = inp
        scores = jnp.einsum('bhnd,bhmd->bhnm', qi, ki) * decay_in[None]
        inner = jnp.einsum('bhnm,bhmv->bhnv', scores, vi)
        cross = jnp.einsum('bhnd,bhdv->bhnv', qi, state) * xi[None, :, :, None]
        new_state = state * gamma_c[None, :, None, None] + jnp.einsum(
            'bhmd,bhmv->bhdv', ki * zeta[None, :, :, None], vi)
        return new_state, inner + cross

    state0 = jnp.zeros((bsz, nh, dk, dv), jnp.float32)
    _, ys = lax.scan(step, state0, (qc, kc, vc))
    return ys.transpose(1, 0, 3, 2, 4).reshape(bsz, seq, nh, dv)


def retention_mixer(x, w_in, gn_g, gn_b, w_out, cos, sin):
    bsz, seq, _ = x.shape
    hv = RET_HEADS * RET_VAL_DIM
    h = x @ w_in
    q, k, v, g = jnp.split(h, [D_MODEL, 2 * D_MODEL, 2 * D_MODEL + hv], axis=-1)
    q = theta_shift(q.reshape(bsz, seq, RET_HEADS, RET_KEY_DIM).astype(jnp.float32), cos, sin)
    k = theta_shift(k.reshape(bsz, seq, RET_HEADS, RET_KEY_DIM).astype(jnp.float32), cos, sin)
    k = k * (RET_KEY_DIM ** -0.5)
    v = v.reshape(bsz, seq, RET_HEADS, RET_VAL_DIM).astype(jnp.float32)
    o = chunkwise_retention(q, k, v)
    mu = jnp.mean(o, axis=-1, keepdims=True)
    oc = o - mu
    var = jnp.mean(oc * oc, axis=-1, keepdims=True)
    o = (oc * lax.rsqrt(var + GN_EPS)).reshape(bsz, seq, hv)
    o = o * gn_g.astype(jnp.float32) + gn_b.astype(jnp.float32)
    o = (jax.nn.silu(g.astype(jnp.float32)) * o).astype(x.dtype)
    return o @ w_out


def stick_breaking_attention(q, k, v):
    seq = q.shape[2]
    scale = q.shape[-1] ** -0.5
    outs = []
    for i in range(seq // SB_BLOCK):
        q0 = i * SB_BLOCK
        end = q0 + SB_BLOCK
        qb = q[:, :, q0:end]
        kp = k[:, :, :end]
        vp = v[:, :, :end]
        z = jnp.einsum('bhqd,bhkd->bhqk', qb, kp).astype(jnp.float32) * scale
        t_idx = q0 + jnp.arange(SB_BLOCK)
        s_idx = jnp.arange(end)
        strict = s_idx[None, :] < t_idx[:, None]
        log_1m = jnp.where(strict, jax.nn.log_sigmoid(-z), 0.0)
        suffix = lax.cumsum(log_1m, axis=3, reverse=True) - log_1m
        a = jnp.where(strict, jnp.exp(jax.nn.log_sigmoid(z) + suffix), 0.0)
        outs.append(jnp.einsum('bhqk,bhkd->bhqd', a.astype(v.dtype), vp))
    return jnp.concatenate(outs, axis=2)


def stick_breaking_mixer(x, w_q, w_out, k_sh, v_sh):
    bsz, seq, _ = x.shape
    q = (x @ w_q).reshape(bsz, seq, SB_HEADS, SB_HEAD_DIM).transpose(0, 2, 1, 3)
    o = stick_breaking_attention(q, k_sh, v_sh)
    o = o.transpose(0, 2, 1, 3).reshape(bsz, seq, D_MODEL)
    return o @ w_out


def causal_depthwise_conv(h, w, b):
    ch = h.shape[-1]
    y = lax.conv_general_dilated(h, w[:, None, :].astype(h.dtype), window_strides=(1,),
                                 padding=[(CONV_WIDTH - 1, 0)],
                                 dimension_numbers=('NWC', 'WIO', 'NWC'),
                                 feature_group_count=ch)
    return y + b


def conv_ffn(x, w_gate, w_up, conv_w, conv_b, w_down):
    gate = causal_depthwise_conv(x @ w_gate, conv_w, conv_b)
    hidden = jax.nn.silu(gate) * (x @ w_up)
    return hidden @ w_down


def setup_inputs(seed: int = 0) -> dict:
    key = jax.random.key(seed)
    ks = jax.random.split(key, 20)
    hv = RET_HEADS * RET_VAL_DIM
    beta = DEEPNORM_BETA
    nrm = jax.random.normal
    f32 = jnp.float32
    x = nrm(ks[0], (BATCH, SEQ, D_MODEL), f32)
    col_scale_a = jnp.concatenate([jnp.ones((2 * D_MODEL,), f32), jnp.full((hv,), beta, f32),
                                   jnp.ones((hv,), f32)]) * (D_MODEL ** -0.5)
    a_w_in = nrm(ks[1], (N_A_LAYERS, D_MODEL, 2 * D_MODEL + 2 * hv), f32) * col_scale_a
    a_gn_g = 1.0 + 0.02 * nrm(ks[2], (N_A_LAYERS, hv), f32)
    a_gn_b = 0.02 * nrm(ks[3], (N_A_LAYERS, hv), f32)
    a_w_out = nrm(ks[4], (N_A_LAYERS, hv, D_MODEL), f32) * (hv ** -0.5) * beta
    b_w_q = nrm(ks[5], (N_B_LAYERS, D_MODEL, D_MODEL), f32) * (D_MODEL ** -0.5)
    b_w_out = nrm(ks[6], (N_B_LAYERS, D_MODEL, D_MODEL), f32) * (D_MODEL ** -0.5) * beta
    col_scale_kv = jnp.concatenate([jnp.ones((D_MODEL,), f32), jnp.full((D_MODEL,), beta, f32)]) * (D_MODEL ** -0.5)
    w_kv = nrm(ks[7], (D_MODEL, 2 * D_MODEL), f32) * col_scale_kv
    ffn_w_gate = nrm(ks[8], (DEPTH, D_MODEL, D_FF), f32) * (D_MODEL ** -0.5) * beta
    ffn_w_up = nrm(ks[9], (DEPTH, D_MODEL, D_FF), f32) * (D_MODEL ** -0.5) * beta
    ffn_conv_w = nrm(ks[10], (DEPTH, CONV_WIDTH, D_FF), f32) * (CONV_WIDTH ** -0.5)
    ffn_conv_b = 0.01 * nrm(ks[11], (DEPTH, D_FF), f32)
    ffn_w_down = nrm(ks[12], (DEPTH, D_FF, D_MODEL), f32) * (D_FF ** -0.5) * beta
    ln_mix_g = 1.0 + 0.02 * nrm(ks[13], (DEPTH, D_MODEL), f32)
    ln_mix_b = 0.02 * nrm(ks[14], (DEPTH, D_MODEL), f32)
    ln_ffn_g = 1.0 + 0.02 * nrm(ks[15], (DEPTH, D_MODEL), f32)
    ln_ffn_b = 0.02 * nrm(ks[16], (DEPTH, D_MODEL), f32)
    return {"x": x, "a_w_in": a_w_in, "a_gn_g": a_gn_g, "a_gn_b": a_gn_b, "a_w_out": a_w_out,
            "b_w_q": b_w_q, "b_w_out": b_w_out, "w_kv": w_kv,
            "ffn_w_gate": ffn_w_gate, "ffn_w_up": ffn_w_up, "ffn_conv_w": ffn_conv_w,
            "ffn_conv_b": ffn_conv_b, "ffn_w_down": ffn_w_down,
            "ln_mix_g": ln_mix_g, "ln_mix_b": ln_mix_b, "ln_ffn_g": ln_ffn_g, "ln_ffn_b": ln_ffn_b}


def reference(x, a_w_in, a_gn_g, a_gn_b, a_w_out, b_w_q, b_w_out, w_kv,
              ffn_w_gate, ffn_w_up, ffn_conv_w, ffn_conv_b, ffn_w_down,
              ln_mix_g, ln_mix_b, ln_ffn_g, ln_ffn_b):
    bsz, seq, _ = x.shape
    cos, sin = rotary_tables(seq)
    k_sh = None
    v_sh = None
    for l in range(DEPTH):
        if l < N_A_LAYERS:
            y = retention_mixer(x, a_w_in[l], a_gn_g[l], a_gn_b[l], a_w_out[l], cos, sin)
        else:
            if l == N_A_LAYERS:
                kv = x @ w_kv
                k_sh, v_sh = jnp.split(kv, 2, axis=-1)
                k_sh = k_sh.reshape(bsz, seq, SB_HEADS, SB_HEAD_DIM).transpose(0, 2, 1, 3)
                v_sh = v_sh.reshape(bsz, seq, SB_HEADS, SB_HEAD_DIM).transpose(0, 2, 1, 3)
            j = l - N_A_LAYERS
            y = stick_breaking_mixer(x, b_w_q[j], b_w_out[j], k_sh, v_sh)
        x = layer_norm(DEEPNORM_ALPHA * x + y, ln_mix_g[l], ln_mix_b[l])
        f = conv_ffn(x, ffn_w_gate[l], ffn_w_up[l], ffn_conv_w[l], ffn_conv_b[l], ffn_w_down[l])
        x = layer_norm(DEEPNORM_ALPHA * x + f, ln_ffn_g[l], ln_ffn_b[l])
    return x
```

```python
import functools

import jax
import jax.numpy as jnp
from jax import lax
from jax.experimental import pallas as pl
from jax.experimental.pallas import tpu as pltpu

RET_KEY_DIM = 256
RET_VAL_DIM = 512
SB_HEAD_DIM = 128
CONV_WIDTH = 3
LN_EPS = 1e-5
GN_EPS = 1e-6
ROPE_BASE = 10000.0
DEPTH = 2
DEEPNORM_ALPHA = (2.0 * DEPTH) ** 0.25

V7X_LANES = 128
V7X_SUBLANES = 8
V7X_VMEM_LIMIT_BYTES = 56 * 1024 * 1024

LOG2E = 1.4426950408889634

_BF16 = jnp.bfloat16
_F32 = jnp.float32


def _pick(total, want):
    t = min(total, want)
    while total % t:
        t -= 1
    return t


def _params(semantics):
    return pltpu.CompilerParams(dimension_semantics=semantics, vmem_limit_bytes=V7X_VMEM_LIMIT_BYTES)


def _layer_norm_rows(r, g, b):
    mu = jnp.mean(r, axis=-1, keepdims=True)
    rc = r - mu
    var = jnp.mean(rc * rc, axis=-1, keepdims=True)
    return rc * lax.rsqrt(var + LN_EPS) * g + b


def _proj_kernel(x_ref, w_ref, cos_ref, sin_ref, o_ref, xb_ref, *, n_q_blocks, n_rot_blocks, k_scale):
    j = pl.program_id(1)

    @pl.when(j == 0)
    def _():
        xb_ref[...] = x_ref[...].astype(_BF16)

    acc = jnp.dot(xb_ref[...], w_ref[...], preferred_element_type=_F32)

    if n_rot_blocks == 0:
        o_ref[...] = acc.astype(o_ref.dtype)
        return

    @pl.when(j < n_rot_blocks)
    def _():
        scale = jnp.where(j >= n_q_blocks, k_scale, 1.0).astype(_F32)
        c = cos_ref[...] * scale
        s = sin_ref[...] * scale
        half = RET_KEY_DIM // 2
        for h in range(acc.shape[1] // RET_KEY_DIM):
            x0 = acc[:, h * RET_KEY_DIM: h * RET_KEY_DIM + half]
            x1 = acc[:, h * RET_KEY_DIM + half: (h + 1) * RET_KEY_DIM]
            o_ref[:, h * RET_KEY_DIM: h * RET_KEY_DIM + half] = (x0 * c - x1 * s).astype(o_ref.dtype)
            o_ref[:, h * RET_KEY_DIM + half: (h + 1) * RET_KEY_DIM] = (x1 * c + x0 * s).astype(o_ref.dtype)

    @pl.when(j >= n_rot_blocks)
    def _():
        o_ref[...] = acc.astype(o_ref.dtype)


def _project(x2d, w_bf, cos, sin, *, seq, rot_cols, q_cols, k_scale, tm_want=1024, tn_want=512):
    t, d = x2d.shape
    n = w_bf.shape[1]
    tm = _pick(seq, tm_want)
    tn = _pick(n, tn_want)
    assert rot_cols % tn == 0 and q_cols % tn == 0 and tn % RET_KEY_DIM == 0
    blocks_per_seq = seq // tm
    kern = functools.partial(_proj_kernel, n_q_blocks=q_cols // tn, n_rot_blocks=rot_cols // tn, k_scale=k_scale)
    half = cos.shape[1]
    return pl.pallas_call(
        kern,
        grid=(t // tm, n // tn),
        in_specs=[
            pl.BlockSpec((tm, d), lambda i, j: (i, 0)),
            pl.BlockSpec((d, tn), lambda i, j: (0, j)),
            pl.BlockSpec((tm, half), lambda i, j: (i % blocks_per_seq, 0)),
            pl.BlockSpec((tm, half), lambda i, j: (i % blocks_per_seq, 0)),
        ],
        out_specs=pl.BlockSpec((tm, tn), lambda i, j: (i, j)),
        out_shape=jax.ShapeDtypeStruct((t, n), _BF16),
        scratch_shapes=[pltpu.VMEM((tm, d), _BF16)],
        compiler_params=_params(("arbitrary", "arbitrary")),
        name="proj",
    )(x2d, w_bf, cos, sin)


def _retention_kernel(gam_ref, q_ref, k_ref, v_ref, g_ref, dec_ref, xi_ref, zeta_ref, gng_ref, gnb_ref,
                      o_ref, state_ref, *, chunk, n_chunks):
    hh = pl.program_id(1)
    c_idx = pl.program_id(2)

    @pl.when(c_idx == 0)
    def _():
        state_ref[...] = jnp.zeros_like(state_ref)

    gamma_c = gam_ref[hh]
    dec = dec_ref[0]
    xi = xi_ref[0]
    zeta = zeta_ref[0]
    gng = gng_ref[...]
    gnb = gnb_ref[...]
    for ci in range(n_chunks):
        rows = pl.ds(ci * chunk, chunk)
        q = q_ref[rows, :]
        k = k_ref[rows, :]
        v = v_ref[rows, :]
        state = state_ref[...]
        scores = lax.dot_general(q, k, (((1,), (1,)), ((), ())), preferred_element_type=_F32) * dec
        inner = jnp.dot(scores.astype(_BF16), v, preferred_element_type=_F32)
        cross = jnp.dot(q, state.astype(_BF16), preferred_element_type=_F32) * xi
        kz = (k.astype(_F32) * zeta).astype(_BF16)
        upd = lax.dot_general(kz, v, (((0,), (0,)), ((), ())), preferred_element_type=_F32)
        state_ref[...] = state * gamma_c + upd
        y = inner + cross
        mu = jnp.mean(y, axis=-1, keepdims=True)
        yc = y - mu
        var = jnp.mean(yc * yc, axis=-1, keepdims=True)
        on = yc * lax.rsqrt(var + GN_EPS) * gng + gnb
        gate = g_ref[rows, :].astype(_F32)
        o_ref[rows, :] = (gate * jax.nn.sigmoid(gate) * on).astype(o_ref.dtype)


def _retention(h, gn_g, gn_b, *, bsz, seq, d_model, heads, chunk=256, rows_want=1024):
    t = h.shape[0]
    hv = heads * RET_VAL_DIM
    chunk = _pick(seq, chunk)
    rows = _pick(seq, rows_want)
    assert rows % chunk == 0
    n_chunks = rows // chunk
    steps = seq // rows

    log_gamma = jnp.log1p(-jnp.exp2(-5.0 - jnp.arange(heads, dtype=_F32)))
    idx = jnp.arange(chunk, dtype=_F32)
    diff = idx[:, None] - idx[None, :]
    dec = jnp.where(diff[None] >= 0.0, jnp.exp(jnp.maximum(diff, 0.0)[None] * log_gamma[:, None, None]), 0.0)
    xi = jnp.exp((idx[None, :] + 1.0) * log_gamma[:, None])[:, :, None]
    zeta = jnp.exp((chunk - 1.0 - idx[None, :]) * log_gamma[:, None])[:, :, None]
    gamma_c = jnp.exp(chunk * log_gamma)

    kq = d_model // RET_KEY_DIM
    vq = 2 * d_model // RET_VAL_DIM
    kern = functools.partial(_retention_kernel, chunk=chunk, n_chunks=n_chunks)
    return pl.pallas_call(
        kern,
        grid=(bsz, heads, steps),
        in_specs=[
            pl.BlockSpec(memory_space=pltpu.SMEM),
            pl.BlockSpec((rows, RET_KEY_DIM), lambda b, hd, c: (b * steps + c, hd)),
            pl.BlockSpec((rows, RET_KEY_DIM), lambda b, hd, c: (b * steps + c, kq + hd)),
            pl.BlockSpec((rows, RET_VAL_DIM), lambda b, hd, c: (b * steps + c, vq + hd)),
            pl.BlockSpec((rows, RET_VAL_DIM), lambda b, hd, c: (b * steps + c, vq + heads + hd)),
            pl.BlockSpec((1, chunk, chunk), lambda b, hd, c: (hd, 0, 0)),
            pl.BlockSpec((1, chunk, 1), lambda b, hd, c: (hd, 0, 0)),
            pl.BlockSpec((1, chunk, 1), lambda b, hd, c: (hd, 0, 0)),
            pl.BlockSpec((1, RET_VAL_DIM), lambda b, hd, c: (0, hd)),
            pl.BlockSpec((1, RET_VAL_DIM), lambda b, hd, c: (0, hd)),
        ],
        out_specs=pl.BlockSpec((rows, RET_VAL_DIM), lambda b, hd, c: (b * steps + c, hd)),
        out_shape=jax.ShapeDtypeStruct((t, hv), _BF16),
        scratch_shapes=[pltpu.VMEM((RET_KEY_DIM, RET_VAL_DIM), _F32)],
        compiler_params=_params(("arbitrary", "arbitrary", "arbitrary")),
        name="retention",
    )(gamma_c, h, h, h, h, dec, xi, zeta, gn_g.reshape(1, hv), gn_b.reshape(1, hv))


def _out_ln_kernel(a_ref, w_ref, x_ref, g_ref, b_ref, o_ref, acc_ref):
    kk = pl.program_id(1)

    @pl.when(kk == 0)
    def _():
        acc_ref[...] = jnp.zeros_like(acc_ref)

    acc_ref[...] += jnp.dot(a_ref[...], w_ref[...], preferred_element_type=_F32)

    @pl.when(kk == pl.num_programs(1) - 1)
    def _():
        r = DEEPNORM_ALPHA * x_ref[...] + acc_ref[...]
        o_ref[...] = _layer_norm_rows(r, g_ref[...], b_ref[...])


def _out_ln(a, w_bf, x2d, g, b, *, tm_want=512, tk_want=1024):
    t, kdim = a.shape
    d = w_bf.shape[1]
    tm = _pick(t, tm_want)
    tk = _pick(kdim, tk_want)
    return pl.pallas_call(
        _out_ln_kernel,
        grid=(t // tm, kdim // tk),
        in_specs=[
            pl.BlockSpec((tm, tk), lambda i, k: (i, k)),
            pl.BlockSpec((tk, d), lambda i, k: (k, 0)),
            pl.BlockSpec((tm, d), lambda i, k: (i, 0)),
            pl.BlockSpec((1, d), lambda i, k: (0, 0)),
            pl.BlockSpec((1, d), lambda i, k: (0, 0)),
        ],
        out_specs=pl.BlockSpec((tm, d), lambda i, k: (i, 0)),
        out_shape=jax.ShapeDtypeStruct((t, d), _F32),
        scratch_shapes=[pltpu.VMEM((tm, d), _F32)],
        compiler_params=_params(("arbitrary", "arbitrary")),
        name="out_ln",
    )(a, w_bf, x2d, g.reshape(1, d), b.reshape(1, d))


def _ffn_kernel(x_ref, wg_ref, wu_ref, cw_ref, cb_ref, wd_ref, g_ref, b_ref, o_ref,
                xb_ref, acc_ref, carry_ref, *, blocks_per_seq, tf):
    i = pl.program_id(0)
    j = pl.program_id(1)

    @pl.when(j == 0)
    def _():
        xb_ref[...] = x_ref[...].astype(_BF16)
        acc_ref[...] = jnp.zeros_like(acc_ref)

    xb = xb_ref[...]
    gp = jnp.dot(xb, wg_ref[...], preferred_element_type=_F32)
    up = jnp.dot(xb, wu_ref[...], preferred_element_type=_F32)

    cols = pl.ds(pl.multiple_of(j * tf, tf), tf)
    seq_start = (i % blocks_per_seq) == 0
    prev = jnp.where(seq_start, 0.0, carry_ref[:, cols])
    tm = gp.shape[0]
    carry_ref[:, cols] = gp[tm - V7X_SUBLANES:, :]

    w0 = cw_ref[0:1, :]
    w1 = cw_ref[1:2, :]
    w2 = cw_ref[2:3, :]
    cb = cb_ref[...]
    body = w0 * pltpu.roll(gp, 2, 0) + w1 * pltpu.roll(gp, 1, 0) + w2 * gp + cb
    ext = jnp.concatenate([prev, gp[:V7X_SUBLANES, :]], axis=0)
    top = (w0 * pltpu.roll(ext, 2, 0) + w1 * pltpu.roll(ext, 1, 0) + w2 * ext + cb)[V7X_SUBLANES:, :]
    gate = jnp.concatenate([top, body[V7X_SUBLANES:, :]], axis=0)
    hidden = (gate * jax.nn.sigmoid(gate) * up).astype(_BF16)
    acc_ref[...] += jnp.dot(hidden, wd_ref[...], preferred_element_type=_F32)

    @pl.when(j == pl.num_programs(1) - 1)
    def _():
        r = DEEPNORM_ALPHA * x_ref[...] + acc_ref[...]
        o_ref[...] = _layer_norm_rows(r, g_ref[...], b_ref[...])


def _conv_ffn_ln(x2d, wg_bf, wu_bf, conv_w, conv_b, wd_bf, g, b, *, seq, tm_want=512, tf_want=512):
    t, d = x2d.shape
    dff = wg_bf.shape[1]
    tm = _pick(seq, tm_want)
    tf = _pick(dff, tf_want)
    assert tm % V7X_SUBLANES == 0 and tm >= 2 * V7X_SUBLANES
    kern = functools.partial(_ffn_kernel, blocks_per_seq=seq // tm, tf=tf)
    return pl.pallas_call(
        kern,
        grid=(t // tm, dff // tf),
        in_specs=[
            pl.BlockSpec((tm, d), lambda i, j: (i, 0)),
            pl.BlockSpec((d, tf), lambda i, j: (0, j)),
            pl.BlockSpec((d, tf), lambda i, j: (0, j)),
            pl.BlockSpec((CONV_WIDTH, tf), lambda i, j: (0, j)),
            pl.BlockSpec((1, tf), lambda i, j: (0, j)),
            pl.BlockSpec((tf, d), lambda i, j: (j, 0)),
            pl.BlockSpec((1, d), lambda i, j: (0, 0)),
            pl.BlockSpec((1, d), lambda i, j: (0, 0)),
        ],
        out_specs=pl.BlockSpec((tm, d), lambda i, j: (i, 0)),
        out_shape=jax.ShapeDtypeStruct((t, d), _F32),
        scratch_shapes=[
            pltpu.VMEM((tm, d), _BF16),
            pltpu.VMEM((tm, d), _F32),
            pltpu.VMEM((V7X_SUBLANES, dff), _F32),
        ],
        compiler_params=_params(("arbitrary", "arbitrary")),
        name="conv_ffn",
    )(x2d, wg_bf, wu_bf, conv_w, conv_b.reshape(1, dff), wd_bf, g.reshape(1, d), b.reshape(1, d))


def _sb_tile(q, ks, vs, tri_neg, c_ref, acc_ref, *, z_to_log2, mask):
    zr = lax.dot_general(q, ks, (((1,), (1,)), ((), ())), preferred_element_type=_F32)
    u = zr * z_to_log2
    e = jnp.exp2(-jnp.abs(u))
    sp = jnp.maximum(u, 0.0) + jnp.log2(1.0 + e)
    if mask is not None:
        sp = jnp.where(mask, sp, 0.0)
    sp_hi = sp.astype(_BF16)
    sp_lo = (sp - sp_hi.astype(_F32)).astype(_BF16)
    suffix = (jnp.dot(sp_hi, tri_neg, preferred_element_type=_F32)
              + jnp.dot(sp_lo, tri_neg, preferred_element_type=_F32))
    p = jnp.exp2(u - sp + suffix)
    if mask is not None:
        p = jnp.where(mask, p, 0.0)
    pv = jnp.dot(p.astype(_BF16), vs, preferred_element_type=_F32)
    c = c_ref[...]
    acc_ref[...] += jnp.exp2(c) * pv
    c_ref[...] = c + (suffix[:, 0:1] - sp[:, 0:1])


def _sb_kernel(q_ref, k_ref, v_ref, o_ref, c_ref, acc_ref, *, tq, tk, z_to_log2):
    i = pl.program_id(2)
    c_ref[...] = jnp.zeros_like(c_ref)
    acc_ref[...] = jnp.zeros_like(acc_ref)
    q = q_ref[...]
    r_idx = lax.broadcasted_iota(jnp.int32, (tk, tk), 0)
    c_idx = lax.broadcasted_iota(jnp.int32, (tk, tk), 1)
    tri_neg = jnp.where(r_idx > c_idx, -1.0, 0.0).astype(_BF16)
    n_diag = tq // tk
    base = i * n_diag
    row = lax.broadcasted_iota(jnp.int32, (tq, tk), 0)
    col = lax.broadcasted_iota(jnp.int32, (tq, tk), 1)
    for dd in range(n_diag - 1, -1, -1):
        start = pl.multiple_of((base + dd) * tk, tk)
        ks = k_ref[pl.ds(start, tk), :]
        vs = v_ref[pl.ds(start, tk), :]
        mask = (col + dd * tk) < row
        _sb_tile(q, ks, vs, tri_neg, c_ref, acc_ref, z_to_log2=z_to_log2, mask=mask)

    def body(jj, carry):
        start = pl.multiple_of((base - 1 - jj) * tk, tk)
        ks = k_ref[pl.ds(start, tk), :]
        vs = v_ref[pl.ds(start, tk), :]
        _sb_tile(q, ks, vs, tri_neg, c_ref, acc_ref, z_to_log2=z_to_log2, mask=None)
        return carry

    lax.fori_loop(0, base, body, 0)
    o_ref[...] = acc_ref[...].astype(o_ref.dtype)


def _stick_breaking(qkv, *, bsz, seq, d_model, tq_want=512, tk_want=256):
    t = qkv.shape[0]
    heads = d_model // SB_HEAD_DIM
    tq = _pick(seq, tq_want)
    tk = _pick(tq, tk_want)
    steps = seq // tq
    z_to_log2 = (SB_HEAD_DIM ** -0.5) * LOG2E
    kern = functools.partial(_sb_kernel, tq=tq, tk=tk, z_to_log2=z_to_log2)
    return pl.pallas_call(
        kern,
        grid=(bsz, heads, steps),
        in_specs=[
            pl.BlockSpec((tq, SB_HEAD_DIM), lambda b, hd, i: (b * steps + i, hd)),
            pl.BlockSpec((seq, SB_HEAD_DIM), lambda b, hd, i: (b, heads + hd)),
            pl.BlockSpec((seq, SB_HEAD_DIM), lambda b, hd, i: (b, 2 * heads + hd)),
        ],
        out_specs=pl.BlockSpec((tq, SB_HEAD_DIM), lambda b, hd, i: (b * steps + i, hd)),
        out_shape=jax.ShapeDtypeStruct((t, d_model), _BF16),
        scratch_shapes=[pltpu.VMEM((tq, 1), _F32), pltpu.VMEM((tq, SB_HEAD_DIM), _F32)],
        compiler_params=_params(("arbitrary", "arbitrary", "arbitrary")),
        name="stick_breaking",
    )(qkv, qkv, qkv)


def _rotary_tables(seq):
    inv = 1.0 / (ROPE_BASE ** jnp.linspace(0.0, 1.0, RET_KEY_DIM // 2, dtype=_F32))
    ang = jnp.arange(seq, dtype=_F32)[:, None] * inv[None, :]
    return jnp.cos(ang), jnp.sin(ang)


def _deinterleave_heads(w_qk):
    d, n = w_qk.shape
    return w_qk.reshape(d, n // RET_KEY_DIM, RET_KEY_DIM // 2, 2).transpose(0, 1, 3, 2).reshape(d, n)


def kernel(x, a_w_in, a_gn_g, a_gn_b, a_w_out, b_w_q, b_w_out, w_kv, ffn_w_gate, ffn_w_up, ffn_conv_w,
           ffn_conv_b, ffn_w_down, ln_mix_g, ln_mix_b, ln_ffn_g, ln_ffn_b):
    bsz, seq, d_model = x.shape
    n_a = a_w_in.shape[0]
    depth = ffn_w_gate.shape[0]
    assert depth == DEPTH
    ret_heads = d_model // RET_KEY_DIM
    t = bsz * seq
    xs = x.reshape(t, d_model)
    cos, sin = _rotary_tables(seq)

    for l in range(depth):
        if l < n_a:
            w_in = a_w_in[l]
            w_perm = jnp.concatenate([_deinterleave_heads(w_in[:, :2 * d_model]), w_in[:, 2 * d_model:]], axis=1)
            h = _project(xs, w_perm.astype(_BF16), cos, sin, seq=seq, rot_cols=2 * d_model, q_cols=d_model,
                         k_scale=RET_KEY_DIM ** -0.5)
            o = _retention(h, a_gn_g[l], a_gn_b[l], bsz=bsz, seq=seq, d_model=d_model, heads=ret_heads)
            xs = _out_ln(o, a_w_out[l].astype(_BF16), xs, ln_mix_g[l], ln_mix_b[l])
        else:
            jl = l - n_a
            if l == n_a:
                w_cat = jnp.concatenate([b_w_q[jl], w_kv], axis=1).astype(_BF16)
                qkv = _project(xs, w_cat, cos, sin, seq=seq, rot_cols=0, q_cols=0, k_scale=1.0)
            else:
                q_new = _project(xs, b_w_q[jl].astype(_BF16), cos, sin, seq=seq, rot_cols=0, q_cols=0, k_scale=1.0)
                qkv = jnp.concatenate([q_new, qkv[:, d_model:]], axis=1)
            o = _stick_breaking(qkv, bsz=bsz, seq=seq, d_model=d_model)
            xs = _out_ln(o, b_w_out[jl].astype(_BF16), xs, ln_mix_g[l], ln_mix_b[l])
        xs = _conv_ffn_ln(xs, ffn_w_gate[l].astype(_BF16), ffn_w_up[l].astype(_BF16), ffn_conv_w[l],
                          ffn_conv_b[l], ffn_w_down[l].astype(_BF16), ln_ffn_g[l], ln_ffn_b[l], seq=seq)
    return xs.reshape(bsz, seq, d_model)
```

```python
import functools

import jax
import jax.numpy as jnp
from jax import lax
from jax.experimental import pallas as pl
from jax.experimental.pallas import tpu as pltpu

RET_KEY_DIM = 256
RET_VAL_DIM = 512
SB_HEAD_DIM = 128
CONV_WIDTH = 3
LN_EPS = 1e-5
GN_EPS = 1e-6
ROPE_BASE = 10000.0
DEPTH = 2
DEEPNORM_ALPHA = (2.0 * DEPTH) ** 0.25

V7X_LANES = 128
V7X_SUBLANES = 8
V7X_VMEM_LIMIT_BYTES = 56 * 1024 * 1024

LOG2E = 1.4426950408889634

_BF16 = jnp.bfloat16
_F32 = jnp.float32


def _pick(total, want):
    t = min(total, want)
    while total % t:
        t -= 1
    return t


def _params(semantics):
    return pltpu.CompilerParams(dimension_semantics=semantics, vmem_limit_bytes=V7X_VMEM_LIMIT_BYTES)


def _layer_norm_rows(r, g, b):
    mu = jnp.mean(r, axis=-1, keepdims=True)
    rc = r - mu
    var = jnp.mean(rc * rc, axis=-1, keepdims=True)
    return rc * lax.rsqrt(var + LN_EPS) * g + b


def _proj_kernel(x_ref, w_ref, cos_ref, sin_ref, o_ref, xb_ref, *, n_q_blocks, n_rot_blocks, k_scale):
    j = pl.program_id(1)

    @pl.when(j == 0)
    def _():
        xb_ref[...] = x_ref[...].astype(_BF16)

    acc = jnp.dot(xb_ref[...], w_ref[...], preferred_element_type=_F32)

    if n_rot_blocks == 0:
        o_ref[...] = acc.astype(o_ref.dtype)
        return

    @pl.when(j < n_rot_blocks)
    def _():
        scale = jnp.where(j >= n_q_blocks, k_scale, 1.0).astype(_F32)
        c = cos_ref[...] * scale
        s = sin_ref[...] * scale
        half = RET_KEY_DIM // 2
        for h in range(acc.shape[1] // RET_KEY_DIM):
            x0 = acc[:, h * RET_KEY_DIM: h * RET_KEY_DIM + half]
            x1 = acc[:, h * RET_KEY_DIM + half: (h + 1) * RET_KEY_DIM]
            o_ref[:, h * RET_KEY_DIM: h * RET_KEY_DIM + half] = (x0 * c - x1 * s).astype(o_ref.dtype)
            o_ref[:, h * RET_KEY_DIM + half: (h + 1) * RET_KEY_DIM] = (x1 * c + x0 * s).astype(o_ref.dtype)

    @pl.when(j >= n_rot_blocks)
    def _():
        o_ref[...] = acc.astype(o_ref.dtype)


def _project(x2d, w_bf, cos, sin, *, seq, rot_cols, q_cols, k_scale, tm_want=1024, tn_want=512):
    t, d = x2d.shape
    n = w_bf.shape[1]
    tm = _pick(seq, tm_want)
    tn = _pick(n, tn_want)
    assert rot_cols % tn == 0 and q_cols % tn == 0 and tn % RET_KEY_DIM == 0
    blocks_per_seq = seq // tm
    kern = functools.partial(_proj_kernel, n_q_blocks=q_cols // tn, n_rot_blocks=rot_cols // tn, k_scale=k_scale)
    half = cos.shape[1]
    return pl.pallas_call(
        kern,
        grid=(t // tm, n // tn),
        in_specs=[
            pl.BlockSpec((tm, d), lambda i, j: (i, 0)),
            pl.BlockSpec((d, tn), lambda i, j: (0, j)),
            pl.BlockSpec((tm, half), lambda i, j: (i % blocks_per_seq, 0)),
            pl.BlockSpec((tm, half), lambda i, j: (i % blocks_per_seq, 0)),
        ],
        out_specs=pl.BlockSpec((tm, tn), lambda i, j: (i, j)),
        out_shape=jax.ShapeDtypeStruct((t, n), _BF16),
        scratch_shapes=[pltpu.VMEM((tm, d), _BF16)],
        compiler_params=_params(("arbitrary", "arbitrary")),
        name="proj",
    )(x2d, w_bf, cos, sin)


def _retention_kernel(gam_ref, q_ref, k_ref, v_ref, g_ref, dec_ref, xi_ref, zeta_ref, gng_ref, gnb_ref,
                      o_ref, state_ref, *, chunk, n_chunks):
    hh = pl.program_id(1)
    c_idx = pl.program_id(2)

    @pl.when(c_idx == 0)
    def _():
        state_ref[...] = jnp.zeros_like(state_ref)

    gamma_c = gam_ref[hh]
    dec = dec_ref[0]
    xi = xi_ref[0]
    zeta = zeta_ref[0]
    gng = gng_ref[...]
    gnb = gnb_ref[...]
    for ci in range(n_chunks):
        rows = pl.ds(ci * chunk, chunk)
        q = q_ref[rows, :]
        k = k_ref[rows, :]
        v = v_ref[rows, :]
        state = state_ref[...]
        scores = lax.dot_general(q, k, (((1,), (1,)), ((), ())), preferred_element_type=_F32) * dec
        inner = jnp.dot(scores.astype(_BF16), v, preferred_element_type=_F32)
        cross = jnp.dot(q, state.astype(_BF16), preferred_element_type=_F32) * xi
        kz = (k.astype(_F32) * zeta).astype(_BF16)
        upd = lax.dot_general(kz, v, (((0,), (0,)), ((), ())), preferred_element_type=_F32)
        state_ref[...] = state * gamma_c + upd
        y = inner + cross
        mu = jnp.mean(y, axis=-1, keepdims=True)
        yc = y - mu
        var = jnp.mean(yc * yc, axis=-1, keepdims=True)
        on = yc * lax.rsqrt(var + GN_EPS) * gng + gnb
        gate = g_ref[rows, :].astype(_F32)
        o_ref[rows, :] = (gate * jax.nn.sigmoid(gate) * on).astype(o_ref.dtype)


def _retention(h, gn_g, gn_b, *, bsz, seq, d_model, heads, chunk=256, rows_want=1024):
    t = h.shape[0]
    hv = heads * RET_VAL_DIM
    chunk = _pick(seq, chunk)
    rows = _pick(seq, rows_want)
    assert rows % chunk == 0
    n_chunks = rows // chunk
    steps = seq // rows

    log_gamma = jnp.log1p(-jnp.exp2(-5.0 - jnp.arange(heads, dtype=_F32)))
    idx = jnp.arange(chunk, dtype=_F32)
    diff = idx[:, None] - idx[None, :]
    dec = jnp.where(diff[None] >= 0.0, jnp.exp(jnp.maximum(diff, 0.0)[None] * log_gamma[:, None, None]), 0.0)
    xi = jnp.exp((idx[None, :] + 1.0) * log_gamma[:, None])[:, :, None]
    zeta = jnp.exp((chunk - 1.0 - idx[None, :]) * log_gamma[:, None])[:, :, None]
    gamma_c = jnp.exp(chunk * log_gamma)

    kq = d_model // RET_KEY_DIM
    vq = 2 * d_model // RET_VAL_DIM
    kern = functools.partial(_retention_kernel, chunk=chunk, n_chunks=n_chunks)
    return pl.pallas_call(
        kern,
        grid=(bsz, heads, steps),
        in_specs=[
            pl.BlockSpec(memory_space=pltpu.SMEM),
            pl.BlockSpec((rows, RET_KEY_DIM), lambda b, hd, c: (b * steps + c, hd)),
            pl.BlockSpec((rows, RET_KEY_DIM), lambda b, hd, c: (b * steps + c, kq + hd)),
            pl.BlockSpec((rows, RET_VAL_DIM), lambda b, hd, c: (b * steps + c, vq + hd)),
            pl.BlockSpec((rows, RET_VAL_DIM), lambda b, hd, c: (b * steps + c, vq + heads + hd)),
            pl.BlockSpec((1, chunk, chunk), lambda b, hd, c: (hd, 0, 0)),
            pl.BlockSpec((1, chunk, 1), lambda b, hd, c: (hd, 0, 0)),
            pl.BlockSpec((1, chunk, 1), lambda b, hd, c: (hd, 0, 0)),
            pl.BlockSpec((1, RET_VAL_DIM), lambda b, hd, c: (0, hd)),
            pl.BlockSpec((1, RET_VAL_DIM), lambda b, hd, c: (0, hd)),
        ],
        out_specs=pl.BlockSpec((rows, RET_VAL_DIM), lambda b, hd, c: (b * steps + c, hd)),
        out_shape=jax.ShapeDtypeStruct((t, hv), _BF16),
        scratch_shapes=[pltpu.VMEM((RET_KEY_DIM, RET_VAL_DIM), _F32)],
        compiler_params=_params(("arbitrary", "arbitrary", "arbitrary")),
        name="retention",
    )(gamma_c, h, h, h, h, dec, xi, zeta, gn_g.reshape(1, hv), gn_b.reshape(1, hv))


def _out_ln_kernel(a_ref, w_ref, x_ref, g_ref, b_ref, o_ref, acc_ref):
    kk = pl.program_id(1)

    @pl.when(kk == 0)
    def _():
        acc_ref[...] = jnp.zeros_like(acc_ref)

    acc_ref[...] += jnp.dot(a_ref[...], w_ref[...], preferred_element_type=_F32)

    @pl.when(kk == pl.num_programs(1) - 1)
    def _():
        r = DEEPNORM_ALPHA * x_ref[...] + acc_ref[...]
        o_ref[...] = _layer_norm_rows(r, g_ref[...], b_ref[...])


def _out_ln(a, w_bf, x2d, g, b, *, tm_want=512, tk_want=1024):
    t, kdim = a.shape
    d = w_bf.shape[1]
    tm = _pick(t, tm_want)
    tk = _pick(kdim, tk_want)
    return pl.pallas_call(
        _out_ln_kernel,
        grid=(t // tm, kdim // tk),
        in_specs=[
            pl.BlockSpec((tm, tk), lambda i, k: (i, k)),
            pl.BlockSpec((tk, d), lambda i, k: (k, 0)),
            pl.BlockSpec((tm, d), lambda i, k: (i, 0)),
            pl.BlockSpec((1, d), lambda i, k: (0, 0)),
            pl.BlockSpec((1, d), lambda i, k: (0, 0)),
        ],
        out_specs=pl.BlockSpec((tm, d), lambda i, k: (i, 0)),
        out_shape=jax.ShapeDtypeStruct((t, d), _F32),
        scratch_shapes=[pltpu.VMEM((tm, d), _F32)],
        compiler_params=_params(("arbitrary", "arbitrary")),
        name="out_ln",
    )(a, w_bf, x2d, g.reshape(1, d), b.reshape(1, d))


def _ffn_kernel(x_ref, wg_ref, wu_ref, cw_ref, cb_ref, wd_ref, g_ref, b_ref, o_ref,
                xb_ref, acc_ref, carry_ref, *, blocks_per_seq, tf):
    i = pl.program_id(0)
    j = pl.program_id(1)

    @pl.when(j == 0)
    def _():
        xb_ref[...] = x_ref[...].astype(_BF16)
        acc_ref[...] = jnp.zeros_like(acc_ref)

    xb = xb_ref[...]
    gp = jnp.dot(xb, wg_ref[...], preferred_element_type=_F32)
    up = jnp.dot(xb, wu_ref[...], preferred_element_type=_F32)

    cols = pl.ds(pl.multiple_of(j * tf, tf), tf)
    seq_start = (i % blocks_per_seq) == 0
    prev = jnp.where(seq_start, 0.0, carry_ref[:, cols])
    tm = gp.shape[0]
    carry_ref[:, cols] = gp[tm - V7X_SUBLANES:, :]

    w0 = cw_ref[0:1, :]
    w1 = cw_ref[1:2, :]
    w2 = cw_ref[2:3, :]
    cb = cb_ref[...]
    body = w0 * pltpu.roll(gp, 2, 0) + w1 * pltpu.roll(gp, 1, 0) + w2 * gp + cb
    ext = jnp.concatenate([prev, gp[:V7X_SUBLANES, :]], axis=0)
    top = (w0 * pltpu.roll(ext, 2, 0) + w1 * pltpu.roll(ext, 1, 0) + w2 * ext + cb)[V7X_SUBLANES:, :]
    gate = jnp.concatenate([top, body[V7X_SUBLANES:, :]], axis=0)
    hidden = (gate * jax.nn.sigmoid(gate) * up).astype(_BF16)
    acc_ref[...] += jnp.dot(hidden, wd_ref[...], preferred_element_type=_F32)

    @pl.when(j == pl.num_programs(1) - 1)
    def _():
        r = DEEPNORM_ALPHA * x_ref[...] + acc_ref[...]
        o_ref[...] = _layer_norm_rows(r, g_ref[...], b_ref[...])


def _conv_ffn_ln(x2d, wg_bf, wu_bf, conv_w, conv_b, wd_bf, g, b, *, seq, tm_want=512, tf_want=512):
    t, d = x2d.shape
    dff = wg_bf.shape[1]
    tm = _pick(seq, tm_want)
    tf = _pick(dff, tf_want)
    assert tm % V7X_SUBLANES == 0 and tm >= 2 * V7X_SUBLANES
    kern = functools.partial(_ffn_kernel, blocks_per_seq=seq // tm, tf=tf)
    return pl.pallas_call(
        kern,
        grid=(t // tm, dff // tf),
        in_specs=[
            pl.BlockSpec((tm, d), lambda i, j: (i, 0)),
            pl.BlockSpec((d, tf), lambda i, j: (0, j)),
            pl.BlockSpec((d, tf), lambda i, j: (0, j)),
            pl.BlockSpec((CONV_WIDTH, tf), lambda i, j: (0, j)),
            pl.BlockSpec((1, tf), lambda i, j: (0, j)),
            pl.BlockSpec((tf, d), lambda i, j: (j, 0)),
            pl.BlockSpec((1, d), lambda i, j: (0, 0)),
            pl.BlockSpec((1, d), lambda i, j: (0, 0)),
        ],
        out_specs=pl.BlockSpec((tm, d), lambda i, j: (i, 0)),
        out_shape=jax.ShapeDtypeStruct((t, d), _F32),
        scratch_shapes=[
            pltpu.VMEM((tm, d), _BF16),
            pltpu.VMEM((tm, d), _F32),
            pltpu.VMEM((V7X_SUBLANES, dff), _F32),
        ],
        compiler_params=_params(("arbitrary", "arbitrary")),
        name="conv_ffn",
    )(x2d, wg_bf, wu_bf, conv_w, conv_b.reshape(1, dff), wd_bf, g.reshape(1, d), b.reshape(1, d))


def _neg_abs(u):
    bits = lax.bitcast_convert_type(u, jnp.uint32) | jnp.uint32(0x80000000)
    return lax.bitcast_convert_type(bits, _F32)


def _split_hi_lo(x):
    hi = lax.bitcast_convert_type(lax.bitcast_convert_type(x, jnp.uint32) & jnp.uint32(0xFFFF0000), _F32)
    return hi.astype(_BF16), (x - hi).astype(_BF16)


_MASKED_LOG2 = -1e30


def _sb_scores(q, k_ref, start, slot, hi_ref, lo_ref, arg_ref, sp0_ref, *, n_sub, tk, mask):
    width = n_sub * tk
    ks = k_ref[pl.ds(start, width), :]
    u = lax.dot_general(q, ks, (((1,), (1,)), ((), ())), preferred_element_type=_F32)
    e = jnp.exp2(_neg_abs(u))
    sp = jnp.maximum(u, 0.0) + jnp.log2(1.0 + e)
    log2_beta = u - sp
    if mask is not None:
        sp = jnp.where(mask, sp, 0.0)
        log2_beta = jnp.where(mask, log2_beta, _MASKED_LOG2)
    sp_hi, sp_lo = _split_hi_lo(sp)
    hi_ref[slot] = sp_hi
    lo_ref[slot] = sp_lo
    arg_ref[slot] = log2_beta
    for g in range(n_sub):
        sp0_ref[slot, g] = sp[:, g * tk: g * tk + 1]


def _sb_weights(v_ref, start, slot, hi_ref, lo_ref, arg_ref, sp0_ref, tri_neg, c_ref, acc_ref, *, n_sub, tk):
    vs = v_ref[pl.ds(start, n_sub * tk), :]
    later = None
    args = [None] * n_sub
    for g in range(n_sub - 1, -1, -1):
        cols = pl.ds(g * tk, tk)
        suffix = (jnp.dot(hi_ref[slot, :, cols], tri_neg, preferred_element_type=_F32)
                  + jnp.dot(lo_ref[slot, :, cols], tri_neg, preferred_element_type=_F32))
        arg = arg_ref[slot, :, cols] + suffix
        total = suffix[:, 0:1] - sp0_ref[slot, g]
        if later is not None:
            arg = arg + later
            total = total + later
        args[g] = arg
        later = total
    p = jnp.exp2(args[0] if n_sub == 1 else jnp.concatenate(args, axis=1))
    pv = jnp.dot(p.astype(_BF16), vs, preferred_element_type=_F32)
    c = c_ref[...]
    acc_ref[...] += jnp.exp2(c) * pv
    c_ref[...] = c + later


def _sb_kernel(q_ref, k_ref, v_ref, o_ref, c_ref, acc_ref, hi_ref, lo_ref, arg_ref, sp0_ref, *, tq, tk):
    i = pl.program_id(2)
    c_ref[...] = jnp.zeros_like(c_ref)
    acc_ref[...] = jnp.zeros_like(acc_ref)
    q = q_ref[...]
    r_idx = lax.broadcasted_iota(jnp.int32, (tk, tk), 0)
    c_idx = lax.broadcasted_iota(jnp.int32, (tk, tk), 1)
    tri_neg = jnp.where(r_idx > c_idx, -1.0, 0.0).astype(_BF16)
    n_sub = tq // tk
    parked = (hi_ref, lo_ref, arg_ref, sp0_ref)
    scores = functools.partial(_sb_scores, q, k_ref, n_sub=n_sub, tk=tk)
    weights = functools.partial(_sb_weights, v_ref, n_sub=n_sub, tk=tk)
    row = lax.broadcasted_iota(jnp.int32, (tq, tq), 0)
    col = lax.broadcasted_iota(jnp.int32, (tq, tq), 1)
    scores(pl.multiple_of(i * tq, tq), 0, *parked, mask=col < row)

    def step(t, slot):
        scores(pl.multiple_of((i - t - 1) * tq, tq), 1 - slot, *parked, mask=None)
        weights(pl.multiple_of((i - t) * tq, tq), slot, *parked, tri_neg, c_ref, acc_ref)

    def body(pair, carry):
        step(2 * pair, 0)
        step(2 * pair + 1, 1)
        return carry

    lax.fori_loop(0, i // 2, body, 0)

    @pl.when(i % 2 == 1)
    def _():
        step(i - 1, 0)
        weights(0, 1, *parked, tri_neg, c_ref, acc_ref)

    @pl.when(i % 2 == 0)
    def _():
        weights(0, 0, *parked, tri_neg, c_ref, acc_ref)

    o_ref[...] = acc_ref[...].astype(o_ref.dtype)


def _stick_breaking(qkv, *, bsz, seq, d_model, tq_want=512, tk_want=256):
    t = qkv.shape[0]
    heads = d_model // SB_HEAD_DIM
    tq = _pick(seq, tq_want)
    tk = _pick(tq, tk_want)
    steps = seq // tq
    kern = functools.partial(_sb_kernel, tq=tq, tk=tk)
    return pl.pallas_call(
        kern,
        grid=(bsz, heads, steps),
        in_specs=[
            pl.BlockSpec((tq, SB_HEAD_DIM), lambda b, hd, i: (b * steps + i, hd)),
            pl.BlockSpec((seq, SB_HEAD_DIM), lambda b, hd, i: (b, heads + hd)),
            pl.BlockSpec((seq, SB_HEAD_DIM), lambda b, hd, i: (b, 2 * heads + hd)),
        ],
        out_specs=pl.BlockSpec((tq, SB_HEAD_DIM), lambda b, hd, i: (b * steps + i, hd)),
        out_shape=jax.ShapeDtypeStruct((t, d_model), _BF16),
        scratch_shapes=[
            pltpu.VMEM((tq, 1), _F32),
            pltpu.VMEM((tq, SB_HEAD_DIM), _F32),
            pltpu.VMEM((2, tq, tq), _BF16),
            pltpu.VMEM((2, tq, tq), _BF16),
            pltpu.VMEM((2, tq, tq), _F32),
            pltpu.VMEM((2, tq // tk, tq, 1), _F32),
        ],
        compiler_params=_params(("arbitrary", "arbitrary", "arbitrary")),
        name="stick_breaking",
    )(qkv, qkv, qkv)


def _rotary_tables(seq):
    inv = 1.0 / (ROPE_BASE ** jnp.linspace(0.0, 1.0, RET_KEY_DIM // 2, dtype=_F32))
    ang = jnp.arange(seq, dtype=_F32)[:, None] * inv[None, :]
    return jnp.cos(ang), jnp.sin(ang)


def _deinterleave_heads(w_qk):
    d, n = w_qk.shape
    return w_qk.reshape(d, n // RET_KEY_DIM, RET_KEY_DIM // 2, 2).transpose(0, 1, 3, 2).reshape(d, n)


def kernel(x, a_w_in, a_gn_g, a_gn_b, a_w_out, b_w_q, b_w_out, w_kv, ffn_w_gate, ffn_w_up, ffn_conv_w,
           ffn_conv_b, ffn_w_down, ln_mix_g, ln_mix_b, ln_ffn_g, ln_ffn_b):
    bsz, seq, d_model = x.shape
    n_a = a_w_in.shape[0]
    depth = ffn_w_gate.shape[0]
    assert depth == DEPTH
    ret_heads = d_model // RET_KEY_DIM
    t = bsz * seq
    xs = x.reshape(t, d_model)
    cos, sin = _rotary_tables(seq)

    for l in range(depth):
        if l < n_a:
            w_in = a_w_in[l]
            w_perm = jnp.concatenate([_deinterleave_heads(w_in[:, :2 * d_model]), w_in[:, 2 * d_model:]], axis=1)
            h = _project(xs, w_perm.astype(_BF16), cos, sin, seq=seq, rot_cols=2 * d_model, q_cols=d_model,
                         k_scale=RET_KEY_DIM ** -0.5)
            o = _retention(h, a_gn_g[l], a_gn_b[l], bsz=bsz, seq=seq, d_model=d_model, heads=ret_heads)
            xs = _out_ln(o, a_w_out[l].astype(_BF16), xs, ln_mix_g[l], ln_mix_b[l])
        else:
            jl = l - n_a
            w_q = b_w_q[jl] * ((SB_HEAD_DIM ** -0.5) * LOG2E)
            if l == n_a:
                w_cat = jnp.concatenate([w_q, w_kv], axis=1).astype(_BF16)
                qkv = _project(xs, w_cat, cos, sin, seq=seq, rot_cols=0, q_cols=0, k_scale=1.0)
            else:
                q_new = _project(xs, w_q.astype(_BF16), cos, sin, seq=seq, rot_cols=0, q_cols=0, k_scale=1.0)
                qkv = jnp.concatenate([q_new, qkv[:, d_model:]], axis=1)
            o = _stick_breaking(qkv, bsz=bsz, seq=seq, d_model=d_model)
            xs = _out_ln(o, b_w_out[jl].astype(_BF16), xs, ln_mix_g[l], ln_mix_b[l])
        xs = _conv_ffn_ln(xs, ffn_w_gate[l].astype(_BF16), ffn_w_up[l].astype(_BF16), ffn_conv_w[l],
                          ffn_conv_b[l], ffn_w_down[l].astype(_BF16), ln_ffn_g[l], ln_ffn_b[l], seq=seq)
    return xs.reshape(bsz, seq, d_model)
```

```python
import functools

import jax
import jax.numpy as jnp
from jax import lax
from jax.experimental import pallas as pl
from jax.experimental.pallas import tpu as pltpu

RET_KEY_DIM = 256
RET_VAL_DIM = 512
SB_HEAD_DIM = 128
CONV_WIDTH = 3
LN_EPS = 1e-5
GN_EPS = 1e-6
ROPE_BASE = 10000.0
DEPTH = 2
DEEPNORM_ALPHA = (2.0 * DEPTH) ** 0.25

V7X_LANES = 128
V7X_SUBLANES = 8
V7X_VMEM_LIMIT_BYTES = 56 * 1024 * 1024

LOG2E = 1.4426950408889634

_BF16 = jnp.bfloat16
_F32 = jnp.float32


def _pick(total, want):
    t = min(total, want)
    while total % t:
        t -= 1
    return t


def _params(semantics):
    return pltpu.CompilerParams(dimension_semantics=semantics, vmem_limit_bytes=V7X_VMEM_LIMIT_BYTES)


def _layer_norm_rows(r, g, b):
    mu = jnp.mean(r, axis=-1, keepdims=True)
    rc = r - mu
    var = jnp.mean(rc * rc, axis=-1, keepdims=True)
    return rc * lax.rsqrt(var + LN_EPS) * g + b


def _proj_kernel(x_ref, w_ref, cos_ref, sin_ref, o_ref, xb_ref, *, n_q_blocks, n_rot_blocks, k_scale):
    j = pl.program_id(1)

    @pl.when(j == 0)
    def _():
        xb_ref[...] = x_ref[...].astype(_BF16)

    acc = jnp.dot(xb_ref[...], w_ref[...], preferred_element_type=_F32)

    if n_rot_blocks == 0:
        o_ref[...] = acc.astype(o_ref.dtype)
        return

    @pl.when(j < n_rot_blocks)
    def _():
        scale = jnp.where(j >= n_q_blocks, k_scale, 1.0).astype(_F32)
        c = cos_ref[...] * scale
        s = sin_ref[...] * scale
        half = RET_KEY_DIM // 2
        for h in range(acc.shape[1] // RET_KEY_DIM):
            x0 = acc[:, h * RET_KEY_DIM: h * RET_KEY_DIM + half]
            x1 = acc[:, h * RET_KEY_DIM + half: (h + 1) * RET_KEY_DIM]
            o_ref[:, h * RET_KEY_DIM: h * RET_KEY_DIM + half] = (x0 * c - x1 * s).astype(o_ref.dtype)
            o_ref[:, h * RET_KEY_DIM + half: (h + 1) * RET_KEY_DIM] = (x1 * c + x0 * s).astype(o_ref.dtype)

    @pl.when(j >= n_rot_blocks)
    def _():
        o_ref[...] = acc.astype(o_ref.dtype)


def _project(x2d, w_bf, cos, sin, *, seq, rot_cols, q_cols, k_scale, tm_want=1024, tn_want=512):
    t, d = x2d.shape
    n = w_bf.shape[1]
    tm = _pick(seq, tm_want)
    tn = _pick(n, tn_want)
    assert rot_cols % tn == 0 and q_cols % tn == 0 and tn % RET_KEY_DIM == 0
    blocks_per_seq = seq // tm
    kern = functools.partial(_proj_kernel, n_q_blocks=q_cols // tn, n_rot_blocks=rot_cols // tn, k_scale=k_scale)
    half = cos.shape[1]
    return pl.pallas_call(
        kern,
        grid=(t // tm, n // tn),
        in_specs=[
            pl.BlockSpec((tm, d), lambda i, j: (i, 0)),
            pl.BlockSpec((d, tn), lambda i, j: (0, j)),
            pl.BlockSpec((tm, half), lambda i, j: (i % blocks_per_seq, 0)),
            pl.BlockSpec((tm, half), lambda i, j: (i % blocks_per_seq, 0)),
        ],
        out_specs=pl.BlockSpec((tm, tn), lambda i, j: (i, j)),
        out_shape=jax.ShapeDtypeStruct((t, n), _BF16),
        scratch_shapes=[pltpu.VMEM((tm, d), _BF16)],
        compiler_params=_params(("arbitrary", "arbitrary")),
        name="proj",
    )(x2d, w_bf, cos, sin)


def _retention_kernel(gam_ref, q_ref, k_ref, v_ref, g_ref, dec_ref, xi_ref, zeta_ref, gng_ref, gnb_ref,
                      o_ref, state_ref, *, chunk, n_chunks):
    hh = pl.program_id(1)
    c_idx = pl.program_id(2)

    @pl.when(c_idx == 0)
    def _():
        state_ref[...] = jnp.zeros_like(state_ref)

    gamma_c = gam_ref[hh]
    dec = dec_ref[0]
    xi = xi_ref[0]
    zeta = zeta_ref[0]
    gng = gng_ref[...]
    gnb = gnb_ref[...]
    for ci in range(n_chunks):
        rows = pl.ds(ci * chunk, chunk)
        q = q_ref[rows, :]
        k = k_ref[rows, :]
        v = v_ref[rows, :]
        state = state_ref[...]
        scores = lax.dot_general(q, k, (((1,), (1,)), ((), ())), preferred_element_type=_F32) * dec
        inner = jnp.dot(scores.astype(_BF16), v, preferred_element_type=_F32)
        cross = jnp.dot(q, state.astype(_BF16), preferred_element_type=_F32) * xi
        kz = (k.astype(_F32) * zeta).astype(_BF16)
        upd = lax.dot_general(kz, v, (((0,), (0,)), ((), ())), preferred_element_type=_F32)
        state_ref[...] = state * gamma_c + upd
        y = inner + cross
        mu = jnp.mean(y, axis=-1, keepdims=True)
        yc = y - mu
        var = jnp.mean(yc * yc, axis=-1, keepdims=True)
        on = yc * lax.rsqrt(var + GN_EPS) * gng + gnb
        gate = g_ref[rows, :].astype(_F32)
        o_ref[rows, :] = (gate * jax.nn.sigmoid(gate) * on).astype(o_ref.dtype)


def _retention(h, gn_g, gn_b, *, bsz, seq, d_model, heads, chunk=256, rows_want=1024):
    t = h.shape[0]
    hv = heads * RET_VAL_DIM
    chunk = _pick(seq, chunk)
    rows = _pick(seq, rows_want)
    assert rows % chunk == 0
    n_chunks = rows // chunk
    steps = seq // rows

    log_gamma = jnp.log1p(-jnp.exp2(-5.0 - jnp.arange(heads, dtype=_F32)))
    idx = jnp.arange(chunk, dtype=_F32)
    diff = idx[:, None] - idx[None, :]
    dec = jnp.where(diff[None] >= 0.0, jnp.exp(jnp.maximum(diff, 0.0)[None] * log_gamma[:, None, None]), 0.0)
    xi = jnp.exp((idx[None, :] + 1.0) * log_gamma[:, None])[:, :, None]
    zeta = jnp.exp((chunk - 1.0 - idx[None, :]) * log_gamma[:, None])[:, :, None]
    gamma_c = jnp.exp(chunk * log_gamma)

    kq = d_model // RET_KEY_DIM
    vq = 2 * d_model // RET_VAL_DIM
    kern = functools.partial(_retention_kernel, chunk=chunk, n_chunks=n_chunks)
    return pl.pallas_call(
        kern,
        grid=(bsz, heads, steps),
        in_specs=[
            pl.BlockSpec(memory_space=pltpu.SMEM),
            pl.BlockSpec((rows, RET_KEY_DIM), lambda b, hd, c: (b * steps + c, hd)),
            pl.BlockSpec((rows, RET_KEY_DIM), lambda b, hd, c: (b * steps + c, kq + hd)),
            pl.BlockSpec((rows, RET_VAL_DIM), lambda b, hd, c: (b * steps + c, vq + hd)),
            pl.BlockSpec((rows, RET_VAL_DIM), lambda b, hd, c: (b * steps + c, vq + heads + hd)),
            pl.BlockSpec((1, chunk, chunk), lambda b, hd, c: (hd, 0, 0)),
            pl.BlockSpec((1, chunk, 1), lambda b, hd, c: (hd, 0, 0)),
            pl.BlockSpec((1, chunk, 1), lambda b, hd, c: (hd, 0, 0)),
            pl.BlockSpec((1, RET_VAL_DIM), lambda b, hd, c: (0, hd)),
            pl.BlockSpec((1, RET_VAL_DIM), lambda b, hd, c: (0, hd)),
        ],
        out_specs=pl.BlockSpec((rows, RET_VAL_DIM), lambda b, hd, c: (b * steps + c, hd)),
        out_shape=jax.ShapeDtypeStruct((t, hv), _BF16),
        scratch_shapes=[pltpu.VMEM((RET_KEY_DIM, RET_VAL_DIM), _F32)],
        compiler_params=_params(("arbitrary", "arbitrary", "arbitrary")),
        name="retention",
    )(gamma_c, h, h, h, h, dec, xi, zeta, gn_g.reshape(1, hv), gn_b.reshape(1, hv))


def _out_ln_kernel(a_ref, w_ref, x_ref, g_ref, b_ref, o_ref, acc_ref):
    kk = pl.program_id(1)

    @pl.when(kk == 0)
    def _():
        acc_ref[...] = jnp.zeros_like(acc_ref)

    acc_ref[...] += jnp.dot(a_ref[...], w_ref[...], preferred_element_type=_F32)

    @pl.when(kk == pl.num_programs(1) - 1)
    def _():
        r = DEEPNORM_ALPHA * x_ref[...] + acc_ref[...]
        o_ref[...] = _layer_norm_rows(r, g_ref[...], b_ref[...])


def _out_ln(a, w_bf, x2d, g, b, *, tm_want=512, tk_want=2048):
    t, kdim = a.shape
    d = w_bf.shape[1]
    tm = _pick(t, tm_want)
    tk = _pick(kdim, tk_want)
    return pl.pallas_call(
        _out_ln_kernel,
        grid=(t // tm, kdim // tk),
        in_specs=[
            pl.BlockSpec((tm, tk), lambda i, k: (i, k)),
            pl.BlockSpec((tk, d), lambda i, k: (k, 0)),
            pl.BlockSpec((tm, d), lambda i, k: (i, 0)),
            pl.BlockSpec((1, d), lambda i, k: (0, 0)),
            pl.BlockSpec((1, d), lambda i, k: (0, 0)),
        ],
        out_specs=pl.BlockSpec((tm, d), lambda i, k: (i, 0)),
        out_shape=jax.ShapeDtypeStruct((t, d), _F32),
        scratch_shapes=[pltpu.VMEM((tm, d), _F32)],
        compiler_params=_params(("arbitrary", "arbitrary")),
        name="out_ln",
    )(a, w_bf, x2d, g.reshape(1, d), b.reshape(1, d))


def _ffn_kernel(x_ref, wg_ref, wu_ref, cw_ref, cb_ref, wd_ref, g_ref, b_ref, o_ref,
                xb_ref, acc_ref, carry_ref, *, blocks_per_seq, tf):
    i = pl.program_id(0)
    j = pl.program_id(1)

    @pl.when(j == 0)
    def _():
        xb_ref[...] = x_ref[...].astype(_BF16)
        acc_ref[...] = jnp.zeros_like(acc_ref)

    xb = xb_ref[...]
    gp = jnp.dot(xb, wg_ref[...], preferred_element_type=_F32)
    up = jnp.dot(xb, wu_ref[...], preferred_element_type=_F32)

    cols = pl.ds(pl.multiple_of(j * tf, tf), tf)
    seq_start = (i % blocks_per_seq) == 0
    prev = jnp.where(seq_start, 0.0, carry_ref[:, cols])
    tm = gp.shape[0]
    carry_ref[:, cols] = gp[tm - V7X_SUBLANES:, :]

    w0 = cw_ref[0:1, :]
    w1 = cw_ref[1:2, :]
    w2 = cw_ref[2:3, :]
    cb = cb_ref[...]
    body = w0 * pltpu.roll(gp, 2, 0) + w1 * pltpu.roll(gp, 1, 0) + w2 * gp + cb
    ext = jnp.concatenate([prev, gp[:V7X_SUBLANES, :]], axis=0)
    top = (w0 * pltpu.roll(ext, 2, 0) + w1 * pltpu.roll(ext, 1, 0) + w2 * ext + cb)[V7X_SUBLANES:, :]
    gate = jnp.concatenate([top, body[V7X_SUBLANES:, :]], axis=0)
    hidden = (gate * jax.nn.sigmoid(gate) * up).astype(_BF16)
    acc_ref[...] += jnp.dot(hidden, wd_ref[...], preferred_element_type=_F32)

    @pl.when(j == pl.num_programs(1) - 1)
    def _():
        r = DEEPNORM_ALPHA * x_ref[...] + acc_ref[...]
        o_ref[...] = _layer_norm_rows(r, g_ref[...], b_ref[...])


def _conv_ffn_ln(x2d, wg_bf, wu_bf, conv_w, conv_b, wd_bf, g, b, *, seq, tm_want=512, tf_want=512):
    t, d = x2d.shape
    dff = wg_bf.shape[1]
    tm = _pick(seq, tm_want)
    tf = _pick(dff, tf_want)
    assert tm % V7X_SUBLANES == 0 and tm >= 2 * V7X_SUBLANES
    kern = functools.partial(_ffn_kernel, blocks_per_seq=seq // tm, tf=tf)
    return pl.pallas_call(
        kern,
        grid=(t // tm, dff // tf),
        in_specs=[
            pl.BlockSpec((tm, d), lambda i, j: (i, 0)),
            pl.BlockSpec((d, tf), lambda i, j: (0, j)),
            pl.BlockSpec((d, tf), lambda i, j: (0, j)),
            pl.BlockSpec((CONV_WIDTH, tf), lambda i, j: (0, j)),
            pl.BlockSpec((1, tf), lambda i, j: (0, j)),
            pl.BlockSpec((tf, d), lambda i, j: (j, 0)),
            pl.BlockSpec((1, d), lambda i, j: (0, 0)),
            pl.BlockSpec((1, d), lambda i, j: (0, 0)),
        ],
        out_specs=pl.BlockSpec((tm, d), lambda i, j: (i, 0)),
        out_shape=jax.ShapeDtypeStruct((t, d), _F32),
        scratch_shapes=[
            pltpu.VMEM((tm, d), _BF16),
            pltpu.VMEM((tm, d), _F32),
            pltpu.VMEM((V7X_SUBLANES, dff), _F32),
        ],
        compiler_params=_params(("arbitrary", "arbitrary")),
        name="conv_ffn",
    )(x2d, wg_bf, wu_bf, conv_w, conv_b.reshape(1, dff), wd_bf, g.reshape(1, d), b.reshape(1, d))


def _neg_abs(u):
    bits = lax.bitcast_convert_type(u, jnp.uint32) | jnp.uint32(0x80000000)
    return lax.bitcast_convert_type(bits, _F32)


def _split_hi_lo(x):
    hi = lax.bitcast_convert_type(lax.bitcast_convert_type(x, jnp.uint32) & jnp.uint32(0xFFFF0000), _F32)
    return hi.astype(_BF16), (x - hi).astype(_BF16)


_MASKED_LOG2 = -1e30


def _sb_scores(q, k_ref, start, slot, hi_ref, lo_ref, arg_ref, sp0_ref, *, n_sub, tk, mask):
    width = n_sub * tk
    ks = k_ref[pl.ds(start, width), :]
    u = lax.dot_general(q, ks, (((1,), (1,)), ((), ())), preferred_element_type=_F32)
    e = jnp.exp2(_neg_abs(u))
    sp = jnp.maximum(u, 0.0) + jnp.log2(1.0 + e)
    log2_beta = u - sp
    if mask is not None:
        sp = jnp.where(mask, sp, 0.0)
        log2_beta = jnp.where(mask, log2_beta, _MASKED_LOG2)
    sp_hi, sp_lo = _split_hi_lo(sp)
    hi_ref[slot] = sp_hi
    lo_ref[slot] = sp_lo
    arg_ref[slot] = log2_beta
    for g in range(n_sub):
        sp0_ref[slot, g] = sp[:, g * tk: g * tk + 1]


def _sb_weights(v_ref, start, slot, hi_ref, lo_ref, arg_ref, sp0_ref, tri_neg, c_ref, acc_ref, *, n_sub, tk):
    vs = v_ref[pl.ds(start, n_sub * tk), :]
    later = None
    args = [None] * n_sub
    for g in range(n_sub - 1, -1, -1):
        cols = pl.ds(g * tk, tk)
        suffix = (jnp.dot(hi_ref[slot, :, cols], tri_neg, preferred_element_type=_F32)
                  + jnp.dot(lo_ref[slot, :, cols], tri_neg, preferred_element_type=_F32))
        arg = arg_ref[slot, :, cols] + suffix
        total = suffix[:, 0:1] - sp0_ref[slot, g]
        if later is not None:
            arg = arg + later
            total = total + later
        args[g] = arg
        later = total
    p = jnp.exp2(args[0] if n_sub == 1 else jnp.concatenate(args, axis=1))
    pv = jnp.dot(p.astype(_BF16), vs, preferred_element_type=_F32)
    c = c_ref[...]
    acc_ref[...] += jnp.exp2(c) * pv
    c_ref[...] = c + later


_STICK_GONE_LOG2 = -160.0


def _sb_kernel(q_ref, k_ref, v_ref, o_ref, c_ref, acc_ref, hi_ref, lo_ref, arg_ref, sp0_ref, *, tq, tb, tk):
    i = pl.program_id(2)
    c_ref[...] = jnp.zeros_like(c_ref)
    acc_ref[...] = jnp.zeros_like(acc_ref)
    q = q_ref[...]
    r_idx = lax.broadcasted_iota(jnp.int32, (tk, tk), 0)
    c_idx = lax.broadcasted_iota(jnp.int32, (tk, tk), 1)
    tri_neg = jnp.where(r_idx > c_idx, -1.0, 0.0).astype(_BF16)
    n_diag = tq // tb
    nb = (i + 1) * n_diag
    parked = (hi_ref, lo_ref, arg_ref, sp0_ref)
    scores = functools.partial(_sb_scores, q, k_ref, n_sub=tb // tk, tk=tk)
    weights = functools.partial(_sb_weights, v_ref, n_sub=tb // tk, tk=tk)
    row = lax.broadcasted_iota(jnp.int32, (tq, tb), 0)
    col = lax.broadcasted_iota(jnp.int32, (tq, tb), 1)

    def key_start(t):
        return pl.multiple_of((nb - 1 - t) * tb, tb)

    def step(t, slot, mask=None):
        scores(key_start(t + 1), 1 - slot, *parked, mask=mask)
        weights(key_start(t), slot, *parked, tri_neg, c_ref, acc_ref)

    def diag_mask(d):
        return col + (tq - (d + 1) * tb) < row

    scores(key_start(0), 0, *parked, mask=diag_mask(0))
    for d in range(n_diag - 1):
        step(d, d % 2, mask=diag_mask(d + 1))

    def alive():
        return jnp.max(c_ref[...]) > _STICK_GONE_LOG2

    def cond(state):
        pair, live = state
        return jnp.logical_and(pair < (nb - n_diag) // 2, live)

    def body(state):
        pair, _ = state
        t = n_diag - 1 + 2 * pair
        step(t, 1)
        step(t + 1, 0)
        return pair + 1, alive()

    pairs_done, live = lax.while_loop(cond, body, (jnp.int32(0), alive()))

    @pl.when(live)
    def _():
        weights(key_start(n_diag - 1 + 2 * pairs_done), 1, *parked, tri_neg, c_ref, acc_ref)

    o_ref[...] = acc_ref[...].astype(o_ref.dtype)


def _stick_breaking(qkv, *, bsz, seq, d_model, tq_want=512, tb_want=256, tk_want=256):
    t = qkv.shape[0]
    heads = d_model // SB_HEAD_DIM
    tq = _pick(seq, tq_want)
    tb = _pick(tq, tb_want)
    tk = _pick(tb, tk_want)
    assert (tq // tb) % 2 == 0, "the band pipeline alternates two parking slots per pair of bands"
    steps = seq // tq
    kern = functools.partial(_sb_kernel, tq=tq, tb=tb, tk=tk)
    return pl.pallas_call(
        kern,
        grid=(bsz, heads, steps),
        in_specs=[
            pl.BlockSpec((tq, SB_HEAD_DIM), lambda b, hd, i: (b * steps + i, hd)),
            pl.BlockSpec((seq, SB_HEAD_DIM), lambda b, hd, i: (b, heads + hd)),
            pl.BlockSpec((seq, SB_HEAD_DIM), lambda b, hd, i: (b, 2 * heads + hd)),
        ],
        out_specs=pl.BlockSpec((tq, SB_HEAD_DIM), lambda b, hd, i: (b * steps + i, hd)),
        out_shape=jax.ShapeDtypeStruct((t, d_model), _BF16),
        scratch_shapes=[
            pltpu.VMEM((tq, 1), _F32),
            pltpu.VMEM((tq, SB_HEAD_DIM), _F32),
            pltpu.VMEM((2, tq, tb), _BF16),
            pltpu.VMEM((2, tq, tb), _BF16),
            pltpu.VMEM((2, tq, tb), _F32),
            pltpu.VMEM((2, tb // tk, tq, 1), _F32),
        ],
        compiler_params=_params(("arbitrary", "arbitrary", "arbitrary")),
        name="stick_breaking",
    )(qkv, qkv, qkv)


def _rotary_tables(seq):
    inv = 1.0 / (ROPE_BASE ** jnp.linspace(0.0, 1.0, RET_KEY_DIM // 2, dtype=_F32))
    ang = jnp.arange(seq, dtype=_F32)[:, None] * inv[None, :]
    return jnp.cos(ang), jnp.sin(ang)


def _deinterleave_heads(w_qk):
    d, n = w_qk.shape
    return w_qk.reshape(d, n // RET_KEY_DIM, RET_KEY_DIM // 2, 2).transpose(0, 1, 3, 2).reshape(d, n)


def kernel(x, a_w_in, a_gn_g, a_gn_b, a_w_out, b_w_q, b_w_out, w_kv, ffn_w_gate, ffn_w_up, ffn_conv_w,
           ffn_conv_b, ffn_w_down, ln_mix_g, ln_mix_b, ln_ffn_g, ln_ffn_b):
    bsz, seq, d_model = x.shape
    n_a = a_w_in.shape[0]
    depth = ffn_w_gate.shape[0]
    assert depth == DEPTH
    ret_heads = d_model // RET_KEY_DIM
    t = bsz * seq
    xs = x.reshape(t, d_model)
    cos, sin = _rotary_tables(seq)

    for l in range(depth):
        if l < n_a:
            w_in = a_w_in[l]
            w_perm = jnp.concatenate([_deinterleave_heads(w_in[:, :2 * d_model]), w_in[:, 2 * d_model:]], axis=1)
            h = _project(xs, w_perm.astype(_BF16), cos, sin, seq=seq, rot_cols=2 * d_model, q_cols=d_model,
                         k_scale=RET_KEY_DIM ** -0.5)
            o = _retention(h, a_gn_g[l], a_gn_b[l], bsz=bsz, seq=seq, d_model=d_model, heads=ret_heads)
            xs = _out_ln(o, a_w_out[l].astype(_BF16), xs, ln_mix_g[l], ln_mix_b[l])
        else:
            jl = l - n_a
            w_q = b_w_q[jl] * ((SB_HEAD_DIM ** -0.5) * LOG2E)
            if l == n_a:
                w_cat = jnp.concatenate([w_q, w_kv], axis=1).astype(_BF16)
                qkv = _project(xs, w_cat, cos, sin, seq=seq, rot_cols=0, q_cols=0, k_scale=1.0)
            else:
                q_new = _project(xs, w_q.astype(_BF16), cos, sin, seq=seq, rot_cols=0, q_cols=0, k_scale=1.0)
                qkv = jnp.concatenate([q_new, qkv[:, d_model:]], axis=1)
            o = _stick_breaking(qkv, bsz=bsz, seq=seq, d_model=d_model)
            xs = _out_ln(o, b_w_out[jl].astype(_BF16), xs, ln_mix_g[l], ln_mix_b[l])
        xs = _conv_ffn_ln(xs, ffn_w_gate[l].astype(_BF16), ffn_w_up[l].astype(_BF16), ffn_conv_w[l],
                          ffn_conv_b[l], ffn_w_down[l].astype(_BF16), ln_ffn_g[l], ln_ffn_b[l], seq=seq)
    return xs.reshape(bsz, seq, d_model)
```

```python
import functools
import math

import jax
import jax.numpy as jnp
from jax import lax
from jax.experimental import pallas as pl
from jax.experimental.pallas import tpu as pltpu

RET_KEY_DIM = 256
RET_VAL_DIM = 512
SB_HEAD_DIM = 128
CONV_WIDTH = 3
LN_EPS = 1e-5
GN_EPS = 1e-6
ROPE_BASE = 10000.0
DEPTH = 2
DEEPNORM_ALPHA = (2.0 * DEPTH) ** 0.25

V7X_LANES = 128
V7X_SUBLANES = 8
V7X_VMEM_LIMIT_BYTES = 56 * 1024 * 1024

LOG2E = 1.4426950408889634

_BF16 = jnp.bfloat16
_F32 = jnp.float32


def _pick(total, want):
    t = min(total, want)
    while total % t:
        t -= 1
    return t


def _params(semantics):
    return pltpu.CompilerParams(dimension_semantics=semantics, vmem_limit_bytes=V7X_VMEM_LIMIT_BYTES)


def _layer_norm_rows(r, g, b):
    mu = jnp.mean(r, axis=-1, keepdims=True)
    rc = r - mu
    var = jnp.mean(rc * rc, axis=-1, keepdims=True)
    return rc * lax.rsqrt(var + LN_EPS) * g + b


def _proj_kernel(x_ref, w_ref, cos_ref, sin_ref, o_ref, xb_ref, *, n_q_blocks, n_rot_blocks, k_scale):
    j = pl.program_id(1)

    @pl.when(j == 0)
    def _():
        xb_ref[...] = x_ref[...].astype(_BF16)

    acc = jnp.dot(xb_ref[...], w_ref[...], preferred_element_type=_F32)

    if n_rot_blocks == 0:
        o_ref[...] = acc.astype(o_ref.dtype)
        return

    @pl.when(j < n_rot_blocks)
    def _():
        scale = jnp.where(j >= n_q_blocks, k_scale, 1.0).astype(_F32)
        c = cos_ref[...] * scale
        s = sin_ref[...] * scale
        half = RET_KEY_DIM // 2
        for h in range(acc.shape[1] // RET_KEY_DIM):
            x0 = acc[:, h * RET_KEY_DIM: h * RET_KEY_DIM + half]
            x1 = acc[:, h * RET_KEY_DIM + half: (h + 1) * RET_KEY_DIM]
            o_ref[:, h * RET_KEY_DIM: h * RET_KEY_DIM + half] = (x0 * c - x1 * s).astype(o_ref.dtype)
            o_ref[:, h * RET_KEY_DIM + half: (h + 1) * RET_KEY_DIM] = (x1 * c + x0 * s).astype(o_ref.dtype)

    @pl.when(j >= n_rot_blocks)
    def _():
        o_ref[...] = acc.astype(o_ref.dtype)


def _project(x2d, w_bf, cos, sin, *, seq, rot_cols, q_cols, k_scale, tm_want=1024, tn_want=1024):
    t, d = x2d.shape
    n = w_bf.shape[1]
    tm = _pick(seq, tm_want)
    tn = _pick(math.gcd(n, q_cols) if rot_cols else n, tn_want)
    assert rot_cols % tn == 0 and q_cols % tn == 0 and tn % RET_KEY_DIM == 0
    blocks_per_seq = seq // tm
    kern = functools.partial(_proj_kernel, n_q_blocks=q_cols // tn, n_rot_blocks=rot_cols // tn, k_scale=k_scale)
    half = cos.shape[1]
    return pl.pallas_call(
        kern,
        grid=(t // tm, n // tn),
        in_specs=[
            pl.BlockSpec((tm, d), lambda i, j: (i, 0)),
            pl.BlockSpec((d, tn), lambda i, j: (0, j)),
            pl.BlockSpec((tm, half), lambda i, j: (i % blocks_per_seq, 0)),
            pl.BlockSpec((tm, half), lambda i, j: (i % blocks_per_seq, 0)),
        ],
        out_specs=pl.BlockSpec((tm, tn), lambda i, j: (i, j)),
        out_shape=jax.ShapeDtypeStruct((t, n), _BF16),
        scratch_shapes=[pltpu.VMEM((tm, d), _BF16)],
        compiler_params=_params(("arbitrary", "arbitrary")),
        name="proj",
    )(x2d, w_bf, cos, sin)


def _retention_kernel(gam_ref, q_ref, k_ref, v_ref, g_ref, dec_ref, xi_ref, zeta_ref, gng_ref, gnb_ref,
                      o_ref, state_ref, *, chunk, n_chunks):
    hh = pl.program_id(1)
    c_idx = pl.program_id(2)

    @pl.when(c_idx == 0)
    def _():
        state_ref[...] = jnp.zeros_like(state_ref)

    gamma_c = gam_ref[hh]
    dec = dec_ref[0]
    xi = xi_ref[0]
    zeta = zeta_ref[0]
    gng = gng_ref[...]
    gnb = gnb_ref[...]
    for ci in range(n_chunks):
        rows = pl.ds(ci * chunk, chunk)
        q = q_ref[rows, :]
        k = k_ref[rows, :]
        v = v_ref[rows, :]
        state = state_ref[...]
        scores = lax.dot_general(q, k, (((1,), (1,)), ((), ())), preferred_element_type=_F32) * dec
        inner = jnp.dot(scores.astype(_BF16), v, preferred_element_type=_F32)
        cross = jnp.dot(q, state.astype(_BF16), preferred_element_type=_F32) * xi
        kz = (k.astype(_F32) * zeta).astype(_BF16)
        upd = lax.dot_general(kz, v, (((0,), (0,)), ((), ())), preferred_element_type=_F32)
        state_ref[...] = state * gamma_c + upd
        y = inner + cross
        mu = jnp.mean(y, axis=-1, keepdims=True)
        yc = y - mu
        var = jnp.mean(yc * yc, axis=-1, keepdims=True)
        on = yc * lax.rsqrt(var + GN_EPS) * gng + gnb
        gate = g_ref[rows, :].astype(_F32)
        o_ref[rows, :] = (gate * jax.nn.sigmoid(gate) * on).astype(o_ref.dtype)


def _retention(h, gn_g, gn_b, *, bsz, seq, d_model, heads, chunk=256, rows_want=2048):
    t = h.shape[0]
    hv = heads * RET_VAL_DIM
    chunk = _pick(seq, chunk)
    rows = _pick(seq, rows_want)
    assert rows % chunk == 0
    n_chunks = rows // chunk
    steps = seq // rows

    log_gamma = jnp.log1p(-jnp.exp2(-5.0 - jnp.arange(heads, dtype=_F32)))
    idx = jnp.arange(chunk, dtype=_F32)
    diff = idx[:, None] - idx[None, :]
    dec = jnp.where(diff[None] >= 0.0, jnp.exp(jnp.maximum(diff, 0.0)[None] * log_gamma[:, None, None]), 0.0)
    xi = jnp.exp((idx[None, :] + 1.0) * log_gamma[:, None])[:, :, None]
    zeta = jnp.exp((chunk - 1.0 - idx[None, :]) * log_gamma[:, None])[:, :, None]
    gamma_c = jnp.exp(chunk * log_gamma)

    kq = d_model // RET_KEY_DIM
    vq = 2 * d_model // RET_VAL_DIM
    kern = functools.partial(_retention_kernel, chunk=chunk, n_chunks=n_chunks)
    return pl.pallas_call(
        kern,
        grid=(bsz, heads, steps),
        in_specs=[
            pl.BlockSpec(memory_space=pltpu.SMEM),
            pl.BlockSpec((rows, RET_KEY_DIM), lambda b, hd, c: (b * steps + c, hd)),
            pl.BlockSpec((rows, RET_KEY_DIM), lambda b, hd, c: (b * steps + c, kq + hd)),
            pl.BlockSpec((rows, RET_VAL_DIM), lambda b, hd, c: (b * steps + c, vq + hd)),
            pl.BlockSpec((rows, RET_VAL_DIM), lambda b, hd, c: (b * steps + c, vq + heads + hd)),
            pl.BlockSpec((1, chunk, chunk), lambda b, hd, c: (hd, 0, 0)),
            pl.BlockSpec((1, chunk, 1), lambda b, hd, c: (hd, 0, 0)),
            pl.BlockSpec((1, chunk, 1), lambda b, hd, c: (hd, 0, 0)),
            pl.BlockSpec((1, RET_VAL_DIM), lambda b, hd, c: (0, hd)),
            pl.BlockSpec((1, RET_VAL_DIM), lambda b, hd, c: (0, hd)),
        ],
        out_specs=pl.BlockSpec((rows, RET_VAL_DIM), lambda b, hd, c: (b * steps + c, hd)),
        out_shape=jax.ShapeDtypeStruct((t, hv), _BF16),
        scratch_shapes=[pltpu.VMEM((RET_KEY_DIM, RET_VAL_DIM), _F32)],
        compiler_params=_params(("arbitrary", "arbitrary", "arbitrary")),
        name="retention",
    )(gamma_c, h, h, h, h, dec, xi, zeta, gn_g.reshape(1, hv), gn_b.reshape(1, hv))


def _out_ln_kernel(a_ref, w_ref, x_ref, g_ref, b_ref, o_ref, acc_ref):
    kk = pl.program_id(1)

    @pl.when(kk == 0)
    def _():
        acc_ref[...] = jnp.zeros_like(acc_ref)

    acc_ref[...] += jnp.dot(a_ref[...], w_ref[...], preferred_element_type=_F32)

    @pl.when(kk == pl.num_programs(1) - 1)
    def _():
        r = DEEPNORM_ALPHA * x_ref[...] + acc_ref[...]
        o_ref[...] = _layer_norm_rows(r, g_ref[...], b_ref[...])


def _out_ln(a, w_bf, x2d, g, b, *, tm_want=512, tk_want=2048):
    t, kdim = a.shape
    d = w_bf.shape[1]
    tm = _pick(t, tm_want)
    tk = _pick(kdim, tk_want)
    return pl.pallas_call(
        _out_ln_kernel,
        grid=(t // tm, kdim // tk),
        in_specs=[
            pl.BlockSpec((tm, tk), lambda i, k: (i, k)),
            pl.BlockSpec((tk, d), lambda i, k: (k, 0)),
            pl.BlockSpec((tm, d), lambda i, k: (i, 0)),
            pl.BlockSpec((1, d), lambda i, k: (0, 0)),
            pl.BlockSpec((1, d), lambda i, k: (0, 0)),
        ],
        out_specs=pl.BlockSpec((tm, d), lambda i, k: (i, 0)),
        out_shape=jax.ShapeDtypeStruct((t, d), _F32),
        scratch_shapes=[pltpu.VMEM((tm, d), _F32)],
        compiler_params=_params(("arbitrary", "arbitrary")),
        name="out_ln",
    )(a, w_bf, x2d, g.reshape(1, d), b.reshape(1, d))


def _ffn_kernel(x_ref, wg_ref, wu_ref, cw_ref, cb_ref, wd_ref, g_ref, b_ref, o_ref,
                xb_ref, acc_ref, carry_ref, *, blocks_per_seq, tf):
    i = pl.program_id(0)
    j = pl.program_id(1)

    @pl.when(j == 0)
    def _():
        xb_ref[...] = x_ref[...].astype(_BF16)
        acc_ref[...] = jnp.zeros_like(acc_ref)

    xb = xb_ref[...]
    gp = jnp.dot(xb, wg_ref[...], preferred_element_type=_F32)
    up = jnp.dot(xb, wu_ref[...], preferred_element_type=_F32)

    cols = pl.ds(pl.multiple_of(j * tf, tf), tf)
    seq_start = (i % blocks_per_seq) == 0
    prev = jnp.where(seq_start, 0.0, carry_ref[:, cols])
    tm = gp.shape[0]
    carry_ref[:, cols] = gp[tm - V7X_SUBLANES:, :]

    w0 = cw_ref[0:1, :]
    w1 = cw_ref[1:2, :]
    w2 = cw_ref[2:3, :]
    cb = cb_ref[...]
    body = w0 * pltpu.roll(gp, 2, 0) + w1 * pltpu.roll(gp, 1, 0) + w2 * gp + cb
    ext = jnp.concatenate([prev, gp[:V7X_SUBLANES, :]], axis=0)
    top = (w0 * pltpu.roll(ext, 2, 0) + w1 * pltpu.roll(ext, 1, 0) + w2 * ext + cb)[V7X_SUBLANES:, :]
    gate = jnp.concatenate([top, body[V7X_SUBLANES:, :]], axis=0)
    hidden = (gate * jax.nn.sigmoid(gate) * up).astype(_BF16)
    acc_ref[...] += jnp.dot(hidden, wd_ref[...], preferred_element_type=_F32)

    @pl.when(j == pl.num_programs(1) - 1)
    def _():
        r = DEEPNORM_ALPHA * x_ref[...] + acc_ref[...]
        o_ref[...] = _layer_norm_rows(r, g_ref[...], b_ref[...])


def _conv_ffn_ln(x2d, wg_bf, wu_bf, conv_w, conv_b, wd_bf, g, b, *, seq, tm_want=512, tf_want=1024):
    t, d = x2d.shape
    dff = wg_bf.shape[1]
    tm = _pick(seq, tm_want)
    tf = _pick(dff, tf_want)
    assert tm % V7X_SUBLANES == 0 and tm >= 2 * V7X_SUBLANES
    kern = functools.partial(_ffn_kernel, blocks_per_seq=seq // tm, tf=tf)
    return pl.pallas_call(
        kern,
        grid=(t // tm, dff // tf),
        in_specs=[
            pl.BlockSpec((tm, d), lambda i, j: (i, 0)),
            pl.BlockSpec((d, tf), lambda i, j: (0, j)),
            pl.BlockSpec((d, tf), lambda i, j: (0, j)),
            pl.BlockSpec((CONV_WIDTH, tf), lambda i, j: (0, j)),
            pl.BlockSpec((1, tf), lambda i, j: (0, j)),
            pl.BlockSpec((tf, d), lambda i, j: (j, 0)),
            pl.BlockSpec((1, d), lambda i, j: (0, 0)),
            pl.BlockSpec((1, d), lambda i, j: (0, 0)),
        ],
        out_specs=pl.BlockSpec((tm, d), lambda i, j: (i, 0)),
        out_shape=jax.ShapeDtypeStruct((t, d), _F32),
        scratch_shapes=[
            pltpu.VMEM((tm, d), _BF16),
            pltpu.VMEM((tm, d), _F32),
            pltpu.VMEM((V7X_SUBLANES, dff), _F32),
        ],
        compiler_params=_params(("arbitrary", "arbitrary")),
        name="conv_ffn",
    )(x2d, wg_bf, wu_bf, conv_w, conv_b.reshape(1, dff), wd_bf, g.reshape(1, d), b.reshape(1, d))


def _neg_abs(u):
    bits = lax.bitcast_convert_type(u, jnp.uint32) | jnp.uint32(0x80000000)
    return lax.bitcast_convert_type(bits, _F32)


def _split_hi_lo(x):
    hi = lax.bitcast_convert_type(lax.bitcast_convert_type(x, jnp.uint32) & jnp.uint32(0xFFFF0000), _F32)
    return hi.astype(_BF16), (x - hi).astype(_BF16)


_MASKED_LOG2 = -1e30


def _sb_scores(q, k_ref, start, slot, *, parked, n_sub, tk, mask):
    hi_ref, lo_ref, arg_ref, sp0_ref = parked
    width = n_sub * tk
    ks = k_ref[pl.ds(start, width), :]
    u = lax.dot_general(q, ks, (((1,), (1,)), ((), ())), preferred_element_type=_F32)
    e = jnp.exp2(_neg_abs(u))
    sp = jnp.maximum(u, 0.0) + jnp.log2(1.0 + e)
    log2_beta = u - sp
    if mask is not None:
        sp = jnp.where(mask, sp, 0.0)
        log2_beta = jnp.where(mask, log2_beta, _MASKED_LOG2)
    sp_hi, sp_lo = _split_hi_lo(sp)
    hi_ref[slot] = sp_hi
    lo_ref[slot] = sp_lo
    arg_ref[slot] = log2_beta
    for g in range(n_sub):
        sp0_ref[slot, g] = sp[:, g * tk: g * tk + 1]


def _sb_weights(v_ref, start, slot, *, parked, tri_neg, c_ref, acc_ref, n_sub, tk):
    hi_ref, lo_ref, arg_ref, sp0_ref = parked
    vs = v_ref[pl.ds(start, n_sub * tk), :]
    later = None
    args = [None] * n_sub
    for g in range(n_sub - 1, -1, -1):
        cols = pl.ds(g * tk, tk)
        suffix = (jnp.dot(hi_ref[slot, :, cols], tri_neg, preferred_element_type=_F32)
                  + jnp.dot(lo_ref[slot, :, cols], tri_neg, preferred_element_type=_F32))
        arg = arg_ref[slot, :, cols] + suffix
        total = suffix[:, 0:1] - sp0_ref[slot, g]
        if later is not None:
            arg = arg + later
            total = total + later
        args[g] = arg
        later = total
    p = jnp.exp2(args[0] if n_sub == 1 else jnp.concatenate(args, axis=1))
    pv = jnp.dot(p.astype(_BF16), vs, preferred_element_type=_F32)
    c = c_ref[...]
    acc_ref[...] += jnp.exp2(c) * pv
    c_ref[...] = c + later


_STICK_GONE_LOG2 = -160.0


def _sb_kernel(q_ref, k_ref, v_ref, o_ref, c_ref, acc_ref, hi_ref, lo_ref, arg_ref, sp0_ref, *, tq, tb, tk, hg):
    i = pl.program_id(2)
    c_ref[...] = jnp.zeros_like(c_ref)
    acc_ref[...] = jnp.zeros_like(acc_ref)
    r_idx = lax.broadcasted_iota(jnp.int32, (tk, tk), 0)
    c_idx = lax.broadcasted_iota(jnp.int32, (tk, tk), 1)
    tri_neg = jnp.where(r_idx > c_idx, -1.0, 0.0).astype(_BF16)
    n_diag = tq // tb
    nb = (i + 1) * n_diag
    row = lax.broadcasted_iota(jnp.int32, (tq, tb), 0)
    col = lax.broadcasted_iota(jnp.int32, (tq, tb), 1)
    dh = SB_HEAD_DIM
    scores, weights = [], []
    for g in range(hg):
        lanes = pl.ds(g * dh, dh)
        parked = (hi_ref.at[g], lo_ref.at[g], arg_ref.at[g], sp0_ref.at[g])
        scores.append(functools.partial(_sb_scores, q_ref[:, g * dh:(g + 1) * dh], k_ref.at[:, lanes],
                                        parked=parked, n_sub=tb // tk, tk=tk))
        weights.append(functools.partial(_sb_weights, v_ref.at[:, lanes], parked=parked, tri_neg=tri_neg,
                                         c_ref=c_ref.at[g], acc_ref=acc_ref.at[g], n_sub=tb // tk, tk=tk))

    def key_start(t):
        return pl.multiple_of((nb - 1 - t) * tb, tb)

    def score_band(t, slot, mask=None):
        for g in range(hg):
            scores[g](key_start(t), slot, mask=mask)

    def finish_band(t, slot):
        for g in range(hg):
            weights[g](key_start(t), slot)

    def step(t, slot, mask=None):
        score_band(t + 1, 1 - slot, mask)
        finish_band(t, slot)

    def diag_mask(d):
        return col + (tq - (d + 1) * tb) < row

    score_band(0, 0, diag_mask(0))
    for d in range(n_diag - 1):
        step(d, d % 2, mask=diag_mask(d + 1))

    def alive():
        return jnp.max(c_ref[...]) > _STICK_GONE_LOG2

    def cond(state):
        pair, live = state
        return jnp.logical_and(pair < (nb - n_diag) // 2, live)

    def body(state):
        pair, _ = state
        t = n_diag - 1 + 2 * pair
        step(t, 1)
        step(t + 1, 0)
        return pair + 1, alive()

    pairs_done, live = lax.while_loop(cond, body, (jnp.int32(0), alive()))

    @pl.when(live)
    def _():
        finish_band(n_diag - 1 + 2 * pairs_done, 1)

    for g in range(hg):
        o_ref[:, g * dh:(g + 1) * dh] = acc_ref[g].astype(o_ref.dtype)


def _stick_breaking(qkv, *, bsz, seq, d_model, tq_want=512, tb_want=256, tk_want=256, heads_per_step=1):
    t = qkv.shape[0]
    heads = d_model // SB_HEAD_DIM
    hg = _pick(heads, heads_per_step)
    tq = _pick(seq, tq_want)
    tb = _pick(tq, tb_want)
    tk = _pick(tb, tk_want)
    assert (tq // tb) % 2 == 0, "the band pipeline alternates two parking slots per pair of bands"
    steps = seq // tq
    groups = heads // hg
    width = hg * SB_HEAD_DIM
    kern = functools.partial(_sb_kernel, tq=tq, tb=tb, tk=tk, hg=hg)
    return pl.pallas_call(
        kern,
        grid=(bsz, groups, steps),
        in_specs=[
            pl.BlockSpec((tq, width), lambda b, hd, i: (b * steps + i, hd)),
            pl.BlockSpec((seq, width), lambda b, hd, i: (b, groups + hd)),
            pl.BlockSpec((seq, width), lambda b, hd, i: (b, 2 * groups + hd)),
        ],
        out_specs=pl.BlockSpec((tq, width), lambda b, hd, i: (b * steps + i, hd)),
        out_shape=jax.ShapeDtypeStruct((t, d_model), _BF16),
        scratch_shapes=[
            pltpu.VMEM((hg, tq, 1), _F32),
            pltpu.VMEM((hg, tq, SB_HEAD_DIM), _F32),
            pltpu.VMEM((hg, 2, tq, tb), _BF16),
            pltpu.VMEM((hg, 2, tq, tb), _BF16),
            pltpu.VMEM((hg, 2, tq, tb), _F32),
            pltpu.VMEM((hg, 2, tb // tk, tq, 1), _F32),
        ],
        compiler_params=_params(("arbitrary", "arbitrary", "arbitrary")),
        name="stick_breaking",
    )(qkv, qkv, qkv)


def _rotary_tables(seq):
    inv = 1.0 / (ROPE_BASE ** jnp.linspace(0.0, 1.0, RET_KEY_DIM // 2, dtype=_F32))
    ang = jnp.arange(seq, dtype=_F32)[:, None] * inv[None, :]
    return jnp.cos(ang), jnp.sin(ang)


def _deinterleave_heads(w_qk):
    d, n = w_qk.shape
    return w_qk.reshape(d, n // RET_KEY_DIM, RET_KEY_DIM // 2, 2).transpose(0, 1, 3, 2).reshape(d, n)


def kernel(x, a_w_in, a_gn_g, a_gn_b, a_w_out, b_w_q, b_w_out, w_kv, ffn_w_gate, ffn_w_up, ffn_conv_w,
           ffn_conv_b, ffn_w_down, ln_mix_g, ln_mix_b, ln_ffn_g, ln_ffn_b):
    bsz, seq, d_model = x.shape
    n_a = a_w_in.shape[0]
    depth = ffn_w_gate.shape[0]
    assert depth == DEPTH
    ret_heads = d_model // RET_KEY_DIM
    t = bsz * seq
    xs = x.reshape(t, d_model)
    cos, sin = _rotary_tables(seq)

    for l in range(depth):
        if l < n_a:
            w_in = a_w_in[l]
            w_perm = jnp.concatenate([_deinterleave_heads(w_in[:, :2 * d_model]), w_in[:, 2 * d_model:]], axis=1)
            h = _project(xs, w_perm.astype(_BF16), cos, sin, seq=seq, rot_cols=2 * d_model, q_cols=d_model,
                         k_scale=RET_KEY_DIM ** -0.5)
            o = _retention(h, a_gn_g[l], a_gn_b[l], bsz=bsz, seq=seq, d_model=d_model, heads=ret_heads)
            xs = _out_ln(o, a_w_out[l].astype(_BF16), xs, ln_mix_g[l], ln_mix_b[l])
        else:
            jl = l - n_a
            w_q = b_w_q[jl] * ((SB_HEAD_DIM ** -0.5) * LOG2E)
            if l == n_a:
                w_cat = jnp.concatenate([w_q, w_kv], axis=1).astype(_BF16)
                qkv = _project(xs, w_cat, cos, sin, seq=seq, rot_cols=0, q_cols=0, k_scale=1.0)
            else:
                q_new = _project(xs, w_q.astype(_BF16), cos, sin, seq=seq, rot_cols=0, q_cols=0, k_scale=1.0)
                qkv = jnp.concatenate([q_new, qkv[:, d_model:]], axis=1)
            o = _stick_breaking(qkv, bsz=bsz, seq=seq, d_model=d_model)
            xs = _out_ln(o, b_w_out[jl].astype(_BF16), xs, ln_mix_g[l], ln_mix_b[l])
        xs = _conv_ffn_ln(xs, ffn_w_gate[l].astype(_BF16), ffn_w_up[l].astype(_BF16), ffn_conv_w[l],
                          ffn_conv_b[l], ffn_w_down[l].astype(_BF16), ln_ffn_g[l], ln_ffn_b[l], seq=seq)
    return xs.reshape(bsz, seq, d_model)
```

```python
import functools
import math

import jax
import jax.numpy as jnp
from jax import lax
from jax.experimental import pallas as pl
from jax.experimental.pallas import tpu as pltpu

RET_KEY_DIM = 256
RET_VAL_DIM = 512
SB_HEAD_DIM = 128
CONV_WIDTH = 3
LN_EPS = 1e-5
GN_EPS = 1e-6
ROPE_BASE = 10000.0
DEPTH = 2
DEEPNORM_ALPHA = (2.0 * DEPTH) ** 0.25

V7X_LANES = 128
V7X_SUBLANES = 8
V7X_VMEM_LIMIT_BYTES = 56 * 1024 * 1024

LOG2E = 1.4426950408889634

_BF16 = jnp.bfloat16
_F32 = jnp.float32


def _pick(total, want):
    t = min(total, want)
    while total % t:
        t -= 1
    return t


def _params(semantics):
    return pltpu.CompilerParams(dimension_semantics=semantics, vmem_limit_bytes=V7X_VMEM_LIMIT_BYTES)


def _layer_norm_rows(r, g, b):
    mu = jnp.mean(r, axis=-1, keepdims=True)
    rc = r - mu
    var = jnp.mean(rc * rc, axis=-1, keepdims=True)
    return rc * lax.rsqrt(var + LN_EPS) * g + b


def _proj_kernel(x_ref, w_ref, cos_ref, sin_ref, o_ref, xb_ref, *, n_q_blocks, n_rot_blocks, k_scale):
    j = pl.program_id(1)

    @pl.when(j == 0)
    def _():
        xb_ref[...] = x_ref[...].astype(_BF16)

    acc = jnp.dot(xb_ref[...], w_ref[...], preferred_element_type=_F32)

    if n_rot_blocks == 0:
        o_ref[...] = acc.astype(o_ref.dtype)
        return

    @pl.when(j < n_rot_blocks)
    def _():
        scale = jnp.where(j >= n_q_blocks, k_scale, 1.0).astype(_F32)
        c = cos_ref[...] * scale
        s = sin_ref[...] * scale
        half = RET_KEY_DIM // 2
        for h in range(acc.shape[1] // RET_KEY_DIM):
            x0 = acc[:, h * RET_KEY_DIM: h * RET_KEY_DIM + half]
            x1 = acc[:, h * RET_KEY_DIM + half: (h + 1) * RET_KEY_DIM]
            o_ref[:, h * RET_KEY_DIM: h * RET_KEY_DIM + half] = (x0 * c - x1 * s).astype(o_ref.dtype)
            o_ref[:, h * RET_KEY_DIM + half: (h + 1) * RET_KEY_DIM] = (x1 * c + x0 * s).astype(o_ref.dtype)

    @pl.when(j >= n_rot_blocks)
    def _():
        o_ref[...] = acc.astype(o_ref.dtype)


def _project(x2d, w_bf, cos, sin, *, seq, rot_cols, q_cols, k_scale, tm_want=1024, tn_want=1024):
    t, d = x2d.shape
    n = w_bf.shape[1]
    tm = _pick(seq, tm_want)
    tn = _pick(math.gcd(n, q_cols) if rot_cols else n, tn_want)
    assert rot_cols % tn == 0 and q_cols % tn == 0 and tn % RET_KEY_DIM == 0
    blocks_per_seq = seq // tm
    kern = functools.partial(_proj_kernel, n_q_blocks=q_cols // tn, n_rot_blocks=rot_cols // tn, k_scale=k_scale)
    half = cos.shape[1]
    return pl.pallas_call(
        kern,
        grid=(t // tm, n // tn),
        in_specs=[
            pl.BlockSpec((tm, d), lambda i, j: (i, 0)),
            pl.BlockSpec((d, tn), lambda i, j: (0, j)),
            pl.BlockSpec((tm, half), lambda i, j: (i % blocks_per_seq, 0)),
            pl.BlockSpec((tm, half), lambda i, j: (i % blocks_per_seq, 0)),
        ],
        out_specs=pl.BlockSpec((tm, tn), lambda i, j: (i, j)),
        out_shape=jax.ShapeDtypeStruct((t, n), _BF16),
        scratch_shapes=[pltpu.VMEM((tm, d), _BF16)],
        compiler_params=_params(("arbitrary", "arbitrary")),
        name="proj",
    )(x2d, w_bf, cos, sin)


def _retention_kernel(gam_ref, q_ref, k_ref, v_ref, g_ref, dec_ref, xi_ref, zeta_ref, gng_ref, gnb_ref,
                      o_ref, state_ref, *, chunk, n_chunks):
    hh = pl.program_id(1)
    c_idx = pl.program_id(2)

    @pl.when(c_idx == 0)
    def _():
        state_ref[...] = jnp.zeros_like(state_ref)

    gamma_c = gam_ref[hh]
    dec = dec_ref[0]
    xi = xi_ref[0]
    zeta = zeta_ref[0]
    gng = gng_ref[...]
    gnb = gnb_ref[...]
    for ci in range(n_chunks):
        rows = pl.ds(ci * chunk, chunk)
        q = q_ref[rows, :]
        k = k_ref[rows, :]
        v = v_ref[rows, :]
        state = state_ref[...]
        scores = lax.dot_general(q, k, (((1,), (1,)), ((), ())), preferred_element_type=_F32) * dec
        inner = jnp.dot(scores.astype(_BF16), v, preferred_element_type=_F32)
        cross = jnp.dot(q, state.astype(_BF16), preferred_element_type=_F32) * xi
        kz = (k.astype(_F32) * zeta).astype(_BF16)
        upd = lax.dot_general(kz, v, (((0,), (0,)), ((), ())), preferred_element_type=_F32)
        state_ref[...] = state * gamma_c + upd
        y = inner + cross
        mu = jnp.mean(y, axis=-1, keepdims=True)
        yc = y - mu
        var = jnp.mean(yc * yc, axis=-1, keepdims=True)
        on = yc * lax.rsqrt(var + GN_EPS) * gng + gnb
        gate = g_ref[rows, :].astype(_F32)
        o_ref[rows, :] = (gate * jax.nn.sigmoid(gate) * on).astype(o_ref.dtype)


def _retention(h, gn_g, gn_b, *, bsz, seq, d_model, heads, chunk=256, rows_want=2048):
    t = h.shape[0]
    hv = heads * RET_VAL_DIM
    chunk = _pick(seq, chunk)
    rows = _pick(seq, rows_want)
    assert rows % chunk == 0
    n_chunks = rows // chunk
    steps = seq // rows

    log_gamma = jnp.log1p(-jnp.exp2(-5.0 - jnp.arange(heads, dtype=_F32)))
    idx = jnp.arange(chunk, dtype=_F32)
    diff = idx[:, None] - idx[None, :]
    dec = jnp.where(diff[None] >= 0.0, jnp.exp(jnp.maximum(diff, 0.0)[None] * log_gamma[:, None, None]), 0.0)
    xi = jnp.exp((idx[None, :] + 1.0) * log_gamma[:, None])[:, :, None]
    zeta = jnp.exp((chunk - 1.0 - idx[None, :]) * log_gamma[:, None])[:, :, None]
    gamma_c = jnp.exp(chunk * log_gamma)

    kq = d_model // RET_KEY_DIM
    vq = 2 * d_model // RET_VAL_DIM
    kern = functools.partial(_retention_kernel, chunk=chunk, n_chunks=n_chunks)
    return pl.pallas_call(
        kern,
        grid=(bsz, heads, steps),
        in_specs=[
            pl.BlockSpec(memory_space=pltpu.SMEM),
            pl.BlockSpec((rows, RET_KEY_DIM), lambda b, hd, c: (b * steps + c, hd)),
            pl.BlockSpec((rows, RET_KEY_DIM), lambda b, hd, c: (b * steps + c, kq + hd)),
            pl.BlockSpec((rows, RET_VAL_DIM), lambda b, hd, c: (b * steps + c, vq + hd)),
            pl.BlockSpec((rows, RET_VAL_DIM), lambda b, hd, c: (b * steps + c, vq + heads + hd)),
            pl.BlockSpec((1, chunk, chunk), lambda b, hd, c: (hd, 0, 0)),
            pl.BlockSpec((1, chunk, 1), lambda b, hd, c: (hd, 0, 0)),
            pl.BlockSpec((1, chunk, 1), lambda b, hd, c: (hd, 0, 0)),
            pl.BlockSpec((1, RET_VAL_DIM), lambda b, hd, c: (0, hd)),
            pl.BlockSpec((1, RET_VAL_DIM), lambda b, hd, c: (0, hd)),
        ],
        out_specs=pl.BlockSpec((rows, RET_VAL_DIM), lambda b, hd, c: (b * steps + c, hd)),
        out_shape=jax.ShapeDtypeStruct((t, hv), _BF16),
        scratch_shapes=[pltpu.VMEM((RET_KEY_DIM, RET_VAL_DIM), _F32)],
        compiler_params=_params(("arbitrary", "arbitrary", "arbitrary")),
        name="retention",
    )(gamma_c, h, h, h, h, dec, xi, zeta, gn_g.reshape(1, hv), gn_b.reshape(1, hv))


def _out_ln_kernel(a_ref, w_ref, x_ref, g_ref, b_ref, o_ref, acc_ref):
    kk = pl.program_id(1)

    @pl.when(kk == 0)
    def _():
        acc_ref[...] = jnp.zeros_like(acc_ref)

    acc_ref[...] += jnp.dot(a_ref[...], w_ref[...], preferred_element_type=_F32)

    @pl.when(kk == pl.num_programs(1) - 1)
    def _():
        r = DEEPNORM_ALPHA * x_ref[...] + acc_ref[...]
        o_ref[...] = _layer_norm_rows(r, g_ref[...], b_ref[...])


def _out_ln(a, w_bf, x2d, g, b, *, tm_want=512, tk_want=2048):
    t, kdim = a.shape
    d = w_bf.shape[1]
    tm = _pick(t, tm_want)
    tk = _pick(kdim, tk_want)
    return pl.pallas_call(
        _out_ln_kernel,
        grid=(t // tm, kdim // tk),
        in_specs=[
            pl.BlockSpec((tm, tk), lambda i, k: (i, k)),
            pl.BlockSpec((tk, d), lambda i, k: (k, 0)),
            pl.BlockSpec((tm, d), lambda i, k: (i, 0)),
            pl.BlockSpec((1, d), lambda i, k: (0, 0)),
            pl.BlockSpec((1, d), lambda i, k: (0, 0)),
        ],
        out_specs=pl.BlockSpec((tm, d), lambda i, k: (i, 0)),
        out_shape=jax.ShapeDtypeStruct((t, d), _F32),
        scratch_shapes=[pltpu.VMEM((tm, d), _F32)],
        compiler_params=_params(("arbitrary", "arbitrary")),
        name="out_ln",
    )(a, w_bf, x2d, g.reshape(1, d), b.reshape(1, d))


def _ffn_kernel(x_ref, wg_ref, wu_ref, cw_ref, cb_ref, wd_ref, g_ref, b_ref, o_ref,
                xb_ref, acc_ref, carry_ref, *, blocks_per_seq, tf):
    i = pl.program_id(0)
    j = pl.program_id(1)

    @pl.when(j == 0)
    def _():
        xb_ref[...] = x_ref[...].astype(_BF16)
        acc_ref[...] = jnp.zeros_like(acc_ref)

    xb = xb_ref[...]
    gp = jnp.dot(xb, wg_ref[...], preferred_element_type=_F32)
    up = jnp.dot(xb, wu_ref[...], preferred_element_type=_F32)

    cols = pl.ds(pl.multiple_of(j * tf, tf), tf)
    seq_start = (i % blocks_per_seq) == 0
    prev = jnp.where(seq_start, 0.0, carry_ref[:, cols])
    tm = gp.shape[0]
    carry_ref[:, cols] = gp[tm - V7X_SUBLANES:, :]

    w0 = cw_ref[0:1, :]
    w1 = cw_ref[1:2, :]
    w2 = cw_ref[2:3, :]
    cb = cb_ref[...]
    body = w0 * pltpu.roll(gp, 2, 0) + w1 * pltpu.roll(gp, 1, 0) + w2 * gp + cb
    ext = jnp.concatenate([prev, gp[:V7X_SUBLANES, :]], axis=0)
    top = (w0 * pltpu.roll(ext, 2, 0) + w1 * pltpu.roll(ext, 1, 0) + w2 * ext + cb)[V7X_SUBLANES:, :]
    gate = jnp.concatenate([top, body[V7X_SUBLANES:, :]], axis=0)
    hidden = (gate * jax.nn.sigmoid(gate) * up).astype(_BF16)
    acc_ref[...] += jnp.dot(hidden, wd_ref[...], preferred_element_type=_F32)

    @pl.when(j == pl.num_programs(1) - 1)
    def _():
        r = DEEPNORM_ALPHA * x_ref[...] + acc_ref[...]
        o_ref[...] = _layer_norm_rows(r, g_ref[...], b_ref[...])


def _conv_ffn_ln(x2d, wg_bf, wu_bf, conv_w, conv_b, wd_bf, g, b, *, seq, tm_want=512, tf_want=1024):
    t, d = x2d.shape
    dff = wg_bf.shape[1]
    tm = _pick(seq, tm_want)
    tf = _pick(dff, tf_want)
    assert tm % V7X_SUBLANES == 0 and tm >= 2 * V7X_SUBLANES
    kern = functools.partial(_ffn_kernel, blocks_per_seq=seq // tm, tf=tf)
    return pl.pallas_call(
        kern,
        grid=(t // tm, dff // tf),
        in_specs=[
            pl.BlockSpec((tm, d), lambda i, j: (i, 0)),
            pl.BlockSpec((d, tf), lambda i, j: (0, j)),
            pl.BlockSpec((d, tf), lambda i, j: (0, j)),
            pl.BlockSpec((CONV_WIDTH, tf), lambda i, j: (0, j)),
            pl.BlockSpec((1, tf), lambda i, j: (0, j)),
            pl.BlockSpec((tf, d), lambda i, j: (j, 0)),
            pl.BlockSpec((1, d), lambda i, j: (0, 0)),
            pl.BlockSpec((1, d), lambda i, j: (0, 0)),
        ],
        out_specs=pl.BlockSpec((tm, d), lambda i, j: (i, 0)),
        out_shape=jax.ShapeDtypeStruct((t, d), _F32),
        scratch_shapes=[
            pltpu.VMEM((tm, d), _BF16),
            pltpu.VMEM((tm, d), _F32),
            pltpu.VMEM((V7X_SUBLANES, dff), _F32),
        ],
        compiler_params=_params(("arbitrary", "arbitrary")),
        name="conv_ffn",
    )(x2d, wg_bf, wu_bf, conv_w, conv_b.reshape(1, dff), wd_bf, g.reshape(1, d), b.reshape(1, d))


def _neg_abs(u):
    bits = lax.bitcast_convert_type(u, jnp.uint32) | jnp.uint32(0x80000000)
    return lax.bitcast_convert_type(bits, _F32)


_MASKED_LOG2 = -1e30


def _sb_scores(q_ref, k_ref, start, slot, *, rows, parked, n_sub, tk, mask):
    sp_ref, arg_ref, sp0_ref = parked
    width = n_sub * tk
    ks = k_ref[pl.ds(start, width), :]
    u = lax.dot_general(q_ref[rows, :], ks, (((1,), (1,)), ((), ())), preferred_element_type=_F32)
    e = jnp.exp2(_neg_abs(u))
    sp = jnp.maximum(u, 0.0) + jnp.log2(1.0 + e)
    log2_beta = u - sp
    if mask is not None:
        sp = jnp.where(mask, sp, 0.0)
        log2_beta = jnp.where(mask, log2_beta, _MASKED_LOG2)
    sp_ref[slot, rows, :] = sp.astype(_BF16)
    arg_ref[slot, rows, :] = log2_beta
    for g in range(n_sub):
        sp0_ref[slot, g, rows, :] = sp[:, g * tk: g * tk + 1]


def _sb_weights(v_ref, start, slot, *, rows, parked, tri_neg, c_ref, acc_ref, n_sub, tk):
    sp_ref, arg_ref, sp0_ref = parked
    vs = v_ref[pl.ds(start, n_sub * tk), :]
    later = None
    args = [None] * n_sub
    for g in range(n_sub - 1, -1, -1):
        cols = pl.ds(g * tk, tk)
        suffix = jnp.dot(sp_ref[slot, rows, cols], tri_neg, preferred_element_type=_F32)
        arg = arg_ref[slot, rows, cols] + suffix
        total = suffix[:, 0:1] - sp0_ref[slot, g, rows, :]
        if later is not None:
            arg = arg + later
            total = total + later
        args[g] = arg
        later = total
    p = jnp.exp2(args[0] if n_sub == 1 else jnp.concatenate(args, axis=1))
    pv = jnp.dot(p.astype(_BF16), vs, preferred_element_type=_F32)
    c = c_ref[rows, :]
    acc_ref[rows, :] += jnp.exp2(c) * pv
    c_ref[rows, :] = c + later


_STICK_GONE_LOG2 = -160.0


def _sb_kernel(q_ref, k_ref, v_ref, o_ref, c_ref, acc_ref, sp_ref, arg_ref, sp0_ref, *, tq, tb, tk, hg):
    i = pl.program_id(2)
    c_ref[...] = jnp.zeros_like(c_ref)
    acc_ref[...] = jnp.zeros_like(acc_ref)
    r_idx = lax.broadcasted_iota(jnp.int32, (tk, tk), 0)
    c_idx = lax.broadcasted_iota(jnp.int32, (tk, tk), 1)
    tri_neg = jnp.where(r_idx > c_idx, -1.0, 0.0).astype(_BF16)
    n_diag = tq // tb
    nb = (i + 1) * n_diag
    row = lax.broadcasted_iota(jnp.int32, (tq, tb), 0)
    col = lax.broadcasted_iota(jnp.int32, (tq, tb), 1)
    dh = SB_HEAD_DIM
    scores, weights = [], []
    for g in range(hg):
        lanes = pl.ds(g * dh, dh)
        parked = (sp_ref.at[g], arg_ref.at[g], sp0_ref.at[g])
        scores.append(functools.partial(_sb_scores, q_ref.at[:, lanes], k_ref.at[:, lanes],
                                        parked=parked, n_sub=tb // tk, tk=tk))
        weights.append(functools.partial(_sb_weights, v_ref.at[:, lanes], parked=parked, tri_neg=tri_neg,
                                         c_ref=c_ref.at[g], acc_ref=acc_ref.at[g], n_sub=tb // tk, tk=tk))

    def key_start(t):
        return pl.multiple_of((nb - 1 - t) * tb, tb)

    def first_row(d):
        return 0 if d is None else tq - (d + 1) * tb

    def score_band(t, slot, d=None):
        r0 = first_row(d)
        mask = None if d is None else (col + r0 < row)[r0:, :]
        for g in range(hg):
            scores[g](key_start(t), slot, rows=pl.ds(r0, tq - r0), mask=mask)

    def finish_band(t, slot, d=None):
        r0 = first_row(d)
        for g in range(hg):
            weights[g](key_start(t), slot, rows=pl.ds(r0, tq - r0))

    def step(t, slot):
        score_band(t + 1, 1 - slot)
        finish_band(t, slot)

    score_band(0, 0, d=0)
    for d in range(n_diag - 1):
        score_band(d + 1, (d + 1) % 2, d=d + 1)
        finish_band(d, d % 2, d=d)

    def alive():
        return jnp.max(c_ref[...]) > _STICK_GONE_LOG2

    def cond(state):
        pair, live = state
        return jnp.logical_and(pair < (nb - n_diag) // 2, live)

    def body(state):
        pair, _ = state
        t = n_diag - 1 + 2 * pair
        step(t, 1)
        step(t + 1, 0)
        return pair + 1, alive()

    pairs_done, live = lax.while_loop(cond, body, (jnp.int32(0), alive()))

    @pl.when(live)
    def _():
        finish_band(n_diag - 1 + 2 * pairs_done, 1)

    for g in range(hg):
        o_ref[:, g * dh:(g + 1) * dh] = acc_ref[g].astype(o_ref.dtype)


def _stick_breaking(qkv, *, bsz, seq, d_model, tq_want=512, tb_want=256, tk_want=256, heads_per_step=1):
    t = qkv.shape[0]
    heads = d_model // SB_HEAD_DIM
    hg = _pick(heads, heads_per_step)
    tq = _pick(seq, tq_want)
    tb = _pick(tq, tb_want)
    tk = _pick(tb, tk_want)
    assert (tq // tb) % 2 == 0, "the band pipeline alternates two parking slots per pair of bands"
    steps = seq // tq
    groups = heads // hg
    width = hg * SB_HEAD_DIM
    kern = functools.partial(_sb_kernel, tq=tq, tb=tb, tk=tk, hg=hg)
    return pl.pallas_call(
        kern,
        grid=(bsz, groups, steps),
        in_specs=[
            pl.BlockSpec((tq, width), lambda b, hd, i: (b * steps + i, hd)),
            pl.BlockSpec((seq, width), lambda b, hd, i: (b, groups + hd)),
            pl.BlockSpec((seq, width), lambda b, hd, i: (b, 2 * groups + hd)),
        ],
        out_specs=pl.BlockSpec((tq, width), lambda b, hd, i: (b * steps + i, hd)),
        out_shape=jax.ShapeDtypeStruct((t, d_model), _BF16),
        scratch_shapes=[
            pltpu.VMEM((hg, tq, 1), _F32),
            pltpu.VMEM((hg, tq, SB_HEAD_DIM), _F32),
            pltpu.VMEM((hg, 2, tq, tb), _BF16),
            pltpu.VMEM((hg, 2, tq, tb), _F32),
            pltpu.VMEM((hg, 2, tb // tk, tq, 1), _F32),
        ],
        compiler_params=_params(("arbitrary", "arbitrary", "arbitrary")),
        name="stick_breaking",
    )(qkv, qkv, qkv)


def _rotary_tables(seq):
    inv = 1.0 / (ROPE_BASE ** jnp.linspace(0.0, 1.0, RET_KEY_DIM // 2, dtype=_F32))
    ang = jnp.arange(seq, dtype=_F32)[:, None] * inv[None, :]
    return jnp.cos(ang), jnp.sin(ang)


def _deinterleave_heads(w_qk):
    d, n = w_qk.shape
    return w_qk.reshape(d, n // RET_KEY_DIM, RET_KEY_DIM // 2, 2).transpose(0, 1, 3, 2).reshape(d, n)


def kernel(x, a_w_in, a_gn_g, a_gn_b, a_w_out, b_w_q, b_w_out, w_kv, ffn_w_gate, ffn_w_up, ffn_conv_w,
           ffn_conv_b, ffn_w_down, ln_mix_g, ln_mix_b, ln_ffn_g, ln_ffn_b):
    bsz, seq, d_model = x.shape
    n_a = a_w_in.shape[0]
    depth = ffn_w_gate.shape[0]
    assert depth == DEPTH
    ret_heads = d_model // RET_KEY_DIM
    t = bsz * seq
    xs = x.reshape(t, d_model)
    cos, sin = _rotary_tables(seq)

    for l in range(depth):
        if l < n_a:
            w_in = a_w_in[l]
            w_perm = jnp.concatenate([_deinterleave_heads(w_in[:, :2 * d_model]), w_in[:, 2 * d_model:]], axis=1)
            h = _project(xs, w_perm.astype(_BF16), cos, sin, seq=seq, rot_cols=2 * d_model, q_cols=d_model,
                         k_scale=RET_KEY_DIM ** -0.5)
            o = _retention(h, a_gn_g[l], a_gn_b[l], bsz=bsz, seq=seq, d_model=d_model, heads=ret_heads)
            xs = _out_ln(o, a_w_out[l].astype(_BF16), xs, ln_mix_g[l], ln_mix_b[l])
        else:
            jl = l - n_a
            w_q = b_w_q[jl] * ((SB_HEAD_DIM ** -0.5) * LOG2E)
            if l == n_a:
                w_cat = jnp.concatenate([w_q, w_kv], axis=1).astype(_BF16)
                qkv = _project(xs, w_cat, cos, sin, seq=seq, rot_cols=0, q_cols=0, k_scale=1.0)
            else:
                q_new = _project(xs, w_q.astype(_BF16), cos, sin, seq=seq, rot_cols=0, q_cols=0, k_scale=1.0)
                qkv = jnp.concatenate([q_new, qkv[:, d_model:]], axis=1)
            o = _stick_breaking(qkv, bsz=bsz, seq=seq, d_model=d_model)
            xs = _out_ln(o, b_w_out[jl].astype(_BF16), xs, ln_mix_g[l], ln_mix_b[l])
        xs = _conv_ffn_ln(xs, ffn_w_gate[l].astype(_BF16), ffn_w_up[l].astype(_BF16), ffn_conv_w[l],
                          ffn_conv_b[l], ffn_w_down[l].astype(_BF16), ln_ffn_g[l], ln_ffn_b[l], seq=seq)
    return xs.reshape(bsz, seq, d_model)
```

```python
import functools
import math

import jax
import jax.numpy as jnp
from jax import lax
from jax.experimental import pallas as pl
from jax.experimental.pallas import tpu as pltpu

RET_KEY_DIM = 256
RET_VAL_DIM = 512
SB_HEAD_DIM = 128
CONV_WIDTH = 3
LN_EPS = 1e-5
GN_EPS = 1e-6
ROPE_BASE = 10000.0
DEPTH = 2
DEEPNORM_ALPHA = (2.0 * DEPTH) ** 0.25

V7X_LANES = 128
V7X_SUBLANES = 8
V7X_VMEM_LIMIT_BYTES = 56 * 1024 * 1024

LOG2E = 1.4426950408889634

_BF16 = jnp.bfloat16
_F32 = jnp.float32


def _pick(total, want):
    t = min(total, want)
    while total % t:
        t -= 1
    return t


def _params(semantics):
    return pltpu.CompilerParams(dimension_semantics=semantics, vmem_limit_bytes=V7X_VMEM_LIMIT_BYTES)


def _layer_norm_rows(r, g, b):
    mu = jnp.mean(r, axis=-1, keepdims=True)
    rc = r - mu
    var = jnp.mean(rc * rc, axis=-1, keepdims=True)
    return rc * lax.rsqrt(var + LN_EPS) * g + b


def _proj_kernel(x_ref, w_ref, cos_ref, sin_ref, o_ref, xb_ref, *, n_q_blocks, n_rot_blocks, k_scale):
    j = pl.program_id(1)

    @pl.when(j == 0)
    def _():
        xb_ref[...] = x_ref[...].astype(_BF16)

    acc = jnp.dot(xb_ref[...], w_ref[...], preferred_element_type=_F32)

    if n_rot_blocks == 0:
        o_ref[...] = acc.astype(o_ref.dtype)
        return

    @pl.when(j < n_rot_blocks)
    def _():
        scale = jnp.where(j >= n_q_blocks, k_scale, 1.0).astype(_F32)
        c = cos_ref[...] * scale
        s = sin_ref[...] * scale
        half = RET_KEY_DIM // 2
        for h in range(acc.shape[1] // RET_KEY_DIM):
            x0 = acc[:, h * RET_KEY_DIM: h * RET_KEY_DIM + half]
            x1 = acc[:, h * RET_KEY_DIM + half: (h + 1) * RET_KEY_DIM]
            o_ref[:, h * RET_KEY_DIM: h * RET_KEY_DIM + half] = (x0 * c - x1 * s).astype(o_ref.dtype)
            o_ref[:, h * RET_KEY_DIM + half: (h + 1) * RET_KEY_DIM] = (x1 * c + x0 * s).astype(o_ref.dtype)

    @pl.when(j >= n_rot_blocks)
    def _():
        o_ref[...] = acc.astype(o_ref.dtype)


def _project(x2d, w_bf, cos, sin, *, seq, rot_cols, q_cols, k_scale, tm_want=1024, tn_want=1024):
    t, d = x2d.shape
    n = w_bf.shape[1]
    tm = _pick(seq, tm_want)
    tn = _pick(math.gcd(n, q_cols) if rot_cols else n, tn_want)
    assert rot_cols % tn == 0 and q_cols % tn == 0 and tn % RET_KEY_DIM == 0
    blocks_per_seq = seq // tm
    kern = functools.partial(_proj_kernel, n_q_blocks=q_cols // tn, n_rot_blocks=rot_cols // tn, k_scale=k_scale)
    half = cos.shape[1]
    return pl.pallas_call(
        kern,
        grid=(t // tm, n // tn),
        in_specs=[
            pl.BlockSpec((tm, d), lambda i, j: (i, 0)),
            pl.BlockSpec((d, tn), lambda i, j: (0, j)),
            pl.BlockSpec((tm, half), lambda i, j: (i % blocks_per_seq, 0)),
            pl.BlockSpec((tm, half), lambda i, j: (i % blocks_per_seq, 0)),
        ],
        out_specs=pl.BlockSpec((tm, tn), lambda i, j: (i, j)),
        out_shape=jax.ShapeDtypeStruct((t, n), _BF16),
        scratch_shapes=[pltpu.VMEM((tm, d), _BF16)],
        compiler_params=_params(("arbitrary", "arbitrary")),
        name="proj",
    )(x2d, w_bf, cos, sin)


def _retention_kernel(gam_ref, q_ref, k_ref, v_ref, g_ref, dec_ref, xi_ref, zeta_ref, gng_ref, gnb_ref,
                      o_ref, state_ref, *, chunk, n_chunks):
    hh = pl.program_id(1)
    c_idx = pl.program_id(2)

    @pl.when(c_idx == 0)
    def _():
        state_ref[...] = jnp.zeros_like(state_ref)

    gamma_c = gam_ref[hh]
    dec = dec_ref[0]
    xi = xi_ref[0]
    zeta = zeta_ref[0]
    gng = gng_ref[...]
    gnb = gnb_ref[...]
    for ci in range(n_chunks):
        rows = pl.ds(ci * chunk, chunk)
        q = q_ref[rows, :]
        k = k_ref[rows, :]
        v = v_ref[rows, :]
        state = state_ref[...]
        scores = lax.dot_general(q, k, (((1,), (1,)), ((), ())), preferred_element_type=_F32) * dec
        inner = jnp.dot(scores.astype(_BF16), v, preferred_element_type=_F32)
        cross = jnp.dot(q, state.astype(_BF16), preferred_element_type=_F32) * xi
        kz = (k.astype(_F32) * zeta).astype(_BF16)
        upd = lax.dot_general(kz, v, (((0,), (0,)), ((), ())), preferred_element_type=_F32)
        state_ref[...] = state * gamma_c + upd
        y = inner + cross
        mu = jnp.mean(y, axis=-1, keepdims=True)
        yc = y - mu
        var = jnp.mean(yc * yc, axis=-1, keepdims=True)
        on = yc * lax.rsqrt(var + GN_EPS) * gng + gnb
        gate = g_ref[rows, :].astype(_F32)
        o_ref[rows, :] = (gate * jax.nn.sigmoid(gate) * on).astype(o_ref.dtype)


def _retention(h, gn_g, gn_b, *, bsz, seq, d_model, heads, chunk=256, rows_want=2048):
    t = h.shape[0]
    hv = heads * RET_VAL_DIM
    chunk = _pick(seq, chunk)
    rows = _pick(seq, rows_want)
    assert rows % chunk == 0
    n_chunks = rows // chunk
    steps = seq // rows

    log_gamma = jnp.log1p(-jnp.exp2(-5.0 - jnp.arange(heads, dtype=_F32)))
    idx = jnp.arange(chunk, dtype=_F32)
    diff = idx[:, None] - idx[None, :]
    dec = jnp.where(diff[None] >= 0.0, jnp.exp(jnp.maximum(diff, 0.0)[None] * log_gamma[:, None, None]), 0.0)
    xi = jnp.exp((idx[None, :] + 1.0) * log_gamma[:, None])[:, :, None]
    zeta = jnp.exp((chunk - 1.0 - idx[None, :]) * log_gamma[:, None])[:, :, None]
    gamma_c = jnp.exp(chunk * log_gamma)

    kq = d_model // RET_KEY_DIM
    vq = 2 * d_model // RET_VAL_DIM
    kern = functools.partial(_retention_kernel, chunk=chunk, n_chunks=n_chunks)
    return pl.pallas_call(
        kern,
        grid=(bsz, heads, steps),
        in_specs=[
            pl.BlockSpec(memory_space=pltpu.SMEM),
            pl.BlockSpec((rows, RET_KEY_DIM), lambda b, hd, c: (b * steps + c, hd)),
            pl.BlockSpec((rows, RET_KEY_DIM), lambda b, hd, c: (b * steps + c, kq + hd)),
            pl.BlockSpec((rows, RET_VAL_DIM), lambda b, hd, c: (b * steps + c, vq + hd)),
            pl.BlockSpec((rows, RET_VAL_DIM), lambda b, hd, c: (b * steps + c, vq + heads + hd)),
            pl.BlockSpec((1, chunk, chunk), lambda b, hd, c: (hd, 0, 0)),
            pl.BlockSpec((1, chunk, 1), lambda b, hd, c: (hd, 0, 0)),
            pl.BlockSpec((1, chunk, 1), lambda b, hd, c: (hd, 0, 0)),
            pl.BlockSpec((1, RET_VAL_DIM), lambda b, hd, c: (0, hd)),
            pl.BlockSpec((1, RET_VAL_DIM), lambda b, hd, c: (0, hd)),
        ],
        out_specs=pl.BlockSpec((rows, RET_VAL_DIM), lambda b, hd, c: (b * steps + c, hd)),
        out_shape=jax.ShapeDtypeStruct((t, hv), _BF16),
        scratch_shapes=[pltpu.VMEM((RET_KEY_DIM, RET_VAL_DIM), _F32)],
        compiler_params=_params(("arbitrary", "arbitrary", "arbitrary")),
        name="retention",
    )(gamma_c, h, h, h, h, dec, xi, zeta, gn_g.reshape(1, hv), gn_b.reshape(1, hv))


def _out_ln_kernel(a_ref, w_ref, x_ref, g_ref, b_ref, o_ref, acc_ref):
    kk = pl.program_id(1)

    @pl.when(kk == 0)
    def _():
        acc_ref[...] = jnp.zeros_like(acc_ref)

    last = pl.num_programs(1) - 1

    @pl.when(kk < last)
    def _():
        acc_ref[...] += jnp.dot(a_ref[...], w_ref[...], preferred_element_type=_F32)

    @pl.when(kk == last)
    def _():
        half = a_ref.shape[0] // 2
        for part in range(2):
            rows = pl.ds(part * half, half)
            y = acc_ref[rows, :] + jnp.dot(a_ref[rows, :], w_ref[...], preferred_element_type=_F32)
            r = DEEPNORM_ALPHA * x_ref[rows, :] + y
            o_ref[rows, :] = _layer_norm_rows(r, g_ref[...], b_ref[...])


def _out_ln(a, w_bf, x2d, g, b, *, tm_want=512, tk_want=2048):
    t, kdim = a.shape
    d = w_bf.shape[1]
    tm = _pick(t, tm_want)
    tk = _pick(kdim, tk_want)
    return pl.pallas_call(
        _out_ln_kernel,
        grid=(t // tm, kdim // tk),
        in_specs=[
            pl.BlockSpec((tm, tk), lambda i, k: (i, k)),
            pl.BlockSpec((tk, d), lambda i, k: (k, 0)),
            pl.BlockSpec((tm, d), lambda i, k: (i, 0)),
            pl.BlockSpec((1, d), lambda i, k: (0, 0)),
            pl.BlockSpec((1, d), lambda i, k: (0, 0)),
        ],
        out_specs=pl.BlockSpec((tm, d), lambda i, k: (i, 0)),
        out_shape=jax.ShapeDtypeStruct((t, d), _F32),
        scratch_shapes=[pltpu.VMEM((tm, d), _F32)],
        compiler_params=_params(("arbitrary", "arbitrary")),
        name="out_ln",
    )(a, w_bf, x2d, g.reshape(1, d), b.reshape(1, d))


def _ffn_kernel(x_ref, wg_ref, wu_ref, cw_ref, cb_ref, wd_ref, g_ref, b_ref, o_ref,
                xb_ref, acc_ref, carry_ref, *, blocks_per_seq, tf):
    i = pl.program_id(0)
    j = pl.program_id(1)

    @pl.when(j == 0)
    def _():
        xb_ref[...] = x_ref[...].astype(_BF16)
        acc_ref[...] = jnp.zeros_like(acc_ref)

    xb = xb_ref[...]
    gp = jnp.dot(xb, wg_ref[...], preferred_element_type=_F32)
    up = jnp.dot(xb, wu_ref[...], preferred_element_type=_F32)

    cols = pl.ds(pl.multiple_of(j * tf, tf), tf)
    seq_start = (i % blocks_per_seq) == 0
    prev = jnp.where(seq_start, 0.0, carry_ref[:, cols])
    tm = gp.shape[0]
    carry_ref[:, cols] = gp[tm - V7X_SUBLANES:, :]

    w0 = cw_ref[0:1, :]
    w1 = cw_ref[1:2, :]
    w2 = cw_ref[2:3, :]
    cb = cb_ref[...]
    body = w0 * pltpu.roll(gp, 2, 0) + w1 * pltpu.roll(gp, 1, 0) + w2 * gp + cb
    ext = jnp.concatenate([prev, gp[:V7X_SUBLANES, :]], axis=0)
    top = (w0 * pltpu.roll(ext, 2, 0) + w1 * pltpu.roll(ext, 1, 0) + w2 * ext + cb)[V7X_SUBLANES:, :]
    gate = jnp.concatenate([top, body[V7X_SUBLANES:, :]], axis=0)
    hidden = (gate * jax.nn.sigmoid(gate) * up).astype(_BF16)
    acc_ref[...] += jnp.dot(hidden, wd_ref[...], preferred_element_type=_F32)

    @pl.when(j == pl.num_programs(1) - 1)
    def _():
        r = DEEPNORM_ALPHA * x_ref[...] + acc_ref[...]
        o_ref[...] = _layer_norm_rows(r, g_ref[...], b_ref[...])


def _conv_ffn_ln(x2d, wg_bf, wu_bf, conv_w, conv_b, wd_bf, g, b, *, layer, seq, tm_want=512, tf_want=1024):
    t, d = x2d.shape
    dff = wg_bf.shape[2]
    tm = _pick(seq, tm_want)
    tf = _pick(dff, tf_want)
    assert tm % V7X_SUBLANES == 0 and tm >= 2 * V7X_SUBLANES
    kern = functools.partial(_ffn_kernel, blocks_per_seq=seq // tm, tf=tf)
    return pl.pallas_call(
        kern,
        grid=(t // tm, dff // tf),
        in_specs=[
            pl.BlockSpec((tm, d), lambda i, j: (i, 0)),
            pl.BlockSpec((None, d, tf), lambda i, j: (layer, 0, j)),
            pl.BlockSpec((None, d, tf), lambda i, j: (layer, 0, j)),
            pl.BlockSpec((CONV_WIDTH, tf), lambda i, j: (0, j)),
            pl.BlockSpec((1, tf), lambda i, j: (0, j)),
            pl.BlockSpec((None, tf, d), lambda i, j: (layer, j, 0)),
            pl.BlockSpec((1, d), lambda i, j: (0, 0)),
            pl.BlockSpec((1, d), lambda i, j: (0, 0)),
        ],
        out_specs=pl.BlockSpec((tm, d), lambda i, j: (i, 0)),
        out_shape=jax.ShapeDtypeStruct((t, d), _F32),
        scratch_shapes=[
            pltpu.VMEM((tm, d), _BF16),
            pltpu.VMEM((tm, d), _F32),
            pltpu.VMEM((V7X_SUBLANES, dff), _F32),
        ],
        compiler_params=_params(("arbitrary", "arbitrary")),
        name="conv_ffn",
    )(x2d, wg_bf, wu_bf, conv_w, conv_b.reshape(1, dff), wd_bf, g.reshape(1, d), b.reshape(1, d))


def _neg_abs(u):
    bits = lax.bitcast_convert_type(u, jnp.uint32) | jnp.uint32(0x80000000)
    return lax.bitcast_convert_type(bits, _F32)


_MASKED_LOG2 = -1e30


def _sb_scores(q_ref, k_ref, start, slot, *, rows, parked, n_sub, tk, mask):
    sp_ref, arg_ref, sp0_ref = parked
    width = n_sub * tk
    ks = k_ref[pl.ds(start, width), :]
    u = lax.dot_general(q_ref[rows, :], ks, (((1,), (1,)), ((), ())), preferred_element_type=_F32)
    e = jnp.exp2(_neg_abs(u))
    sp = jnp.maximum(u, 0.0) + jnp.log2(1.0 + e)
    log2_beta = u - sp
    if mask is not None:
        sp = jnp.where(mask, sp, 0.0)
        log2_beta = jnp.where(mask, log2_beta, _MASKED_LOG2)
    sp_ref[slot, rows, :] = sp.astype(_BF16)
    arg_ref[slot, rows, :] = log2_beta
    for g in range(n_sub):
        sp0_ref[slot, g, rows, :] = sp[:, g * tk: g * tk + 1]


def _sb_weights(v_ref, start, slot, *, rows, parked, tri_neg, c_ref, acc_ref, n_sub, tk):
    sp_ref, arg_ref, sp0_ref = parked
    vs = v_ref[pl.ds(start, n_sub * tk), :]
    later = None
    args = [None] * n_sub
    for g in range(n_sub - 1, -1, -1):
        cols = pl.ds(g * tk, tk)
        suffix = jnp.dot(sp_ref[slot, rows, cols], tri_neg, preferred_element_type=_F32)
        arg = arg_ref[slot, rows, cols] + suffix
        total = suffix[:, 0:1] - sp0_ref[slot, g, rows, :]
        if later is not None:
            arg = arg + later
            total = total + later
        args[g] = arg
        later = total
    p = jnp.exp2(args[0] if n_sub == 1 else jnp.concatenate(args, axis=1))
    pv = jnp.dot(p.astype(_BF16), vs, preferred_element_type=_F32)
    c = c_ref[rows, :]
    acc_ref[rows, :] += jnp.exp2(c) * pv
    c_ref[rows, :] = c + later


_STICK_GONE_LOG2 = -160.0


def _sb_kernel(q_ref, k_ref, v_ref, o_ref, c_ref, acc_ref, sp_ref, arg_ref, sp0_ref, *, tq, tb, tk, hg):
    i = pl.program_id(2)
    c_ref[...] = jnp.zeros_like(c_ref)
    acc_ref[...] = jnp.zeros_like(acc_ref)
    r_idx = lax.broadcasted_iota(jnp.int32, (tk, tk), 0)
    c_idx = lax.broadcasted_iota(jnp.int32, (tk, tk), 1)
    tri_neg = jnp.where(r_idx > c_idx, -1.0, 0.0).astype(_BF16)
    n_diag = tq // tb
    nb = (i + 1) * n_diag
    row = lax.broadcasted_iota(jnp.int32, (tq, tb), 0)
    col = lax.broadcasted_iota(jnp.int32, (tq, tb), 1)
    dh = SB_HEAD_DIM
    scores, weights = [], []
    for g in range(hg):
        lanes = pl.ds(g * dh, dh)
        parked = (sp_ref.at[g], arg_ref.at[g], sp0_ref.at[g])
        scores.append(functools.partial(_sb_scores, q_ref.at[:, lanes], k_ref.at[:, lanes],
                                        parked=parked, n_sub=tb // tk, tk=tk))
        weights.append(functools.partial(_sb_weights, v_ref.at[:, lanes], parked=parked, tri_neg=tri_neg,
                                         c_ref=c_ref.at[g], acc_ref=acc_ref.at[g], n_sub=tb // tk, tk=tk))

    def key_start(t):
        return pl.multiple_of((nb - 1 - t) * tb, tb)

    def first_row(d):
        return 0 if d is None else tq - (d + 1) * tb

    def score_band(t, slot, d=None):
        r0 = first_row(d)
        mask = None if d is None else (col + r0 < row)[r0:, :]
        for g in range(hg):
            scores[g](key_start(t), slot, rows=pl.ds(r0, tq - r0), mask=mask)

    def finish_band(t, slot, d=None):
        r0 = first_row(d)
        for g in range(hg):
            weights[g](key_start(t), slot, rows=pl.ds(r0, tq - r0))

    def step(t, slot):
        score_band(t + 1, 1 - slot)
        finish_band(t, slot)

    score_band(0, 0, d=0)
    for d in range(n_diag - 1):
        score_band(d + 1, (d + 1) % 2, d=d + 1)
        finish_band(d, d % 2, d=d)

    def alive():
        return jnp.max(c_ref[...]) > _STICK_GONE_LOG2

    def cond(state):
        pair, live = state
        return jnp.logical_and(pair < (nb - n_diag) // 2, live)

    def body(state):
        pair, _ = state
        t = n_diag - 1 + 2 * pair
        step(t, 1)
        step(t + 1, 0)
        return pair + 1, alive()

    pairs_done, live = lax.while_loop(cond, body, (jnp.int32(0), alive()))

    @pl.when(live)
    def _():
        finish_band(n_diag - 1 + 2 * pairs_done, 1)

    for g in range(hg):
        o_ref[:, g * dh:(g + 1) * dh] = acc_ref[g].astype(o_ref.dtype)


def _stick_breaking(qkv, *, bsz, seq, d_model, tq_want=512, tb_want=256, tk_want=256, heads_per_step=1):
    t = qkv.shape[0]
    heads = d_model // SB_HEAD_DIM
    hg = _pick(heads, heads_per_step)
    tq = _pick(seq, tq_want)
    tb = _pick(tq, tb_want)
    tk = _pick(tb, tk_want)
    assert (tq // tb) % 2 == 0, "the band pipeline alternates two parking slots per pair of bands"
    steps = seq // tq
    groups = heads // hg
    width = hg * SB_HEAD_DIM
    kern = functools.partial(_sb_kernel, tq=tq, tb=tb, tk=tk, hg=hg)
    return pl.pallas_call(
        kern,
        grid=(bsz, groups, steps),
        in_specs=[
            pl.BlockSpec((tq, width), lambda b, hd, i: (b * steps + i, hd)),
            pl.BlockSpec((seq, width), lambda b, hd, i: (b, groups + hd)),
            pl.BlockSpec((seq, width), lambda b, hd, i: (b, 2 * groups + hd)),
        ],
        out_specs=pl.BlockSpec((tq, width), lambda b, hd, i: (b * steps + i, hd)),
        out_shape=jax.ShapeDtypeStruct((t, d_model), _BF16),
        scratch_shapes=[
            pltpu.VMEM((hg, tq, 1), _F32),
            pltpu.VMEM((hg, tq, SB_HEAD_DIM), _F32),
            pltpu.VMEM((hg, 2, tq, tb), _BF16),
            pltpu.VMEM((hg, 2, tq, tb), _F32),
            pltpu.VMEM((hg, 2, tb // tk, tq, 1), _F32),
        ],
        compiler_params=_params(("arbitrary", "arbitrary", "arbitrary")),
        name="stick_breaking",
    )(qkv, qkv, qkv)


def _rotary_tables(seq):
    inv = 1.0 / (ROPE_BASE ** jnp.linspace(0.0, 1.0, RET_KEY_DIM // 2, dtype=_F32))
    ang = jnp.arange(seq, dtype=_F32)[:, None] * inv[None, :]
    return jnp.cos(ang), jnp.sin(ang)


def _deinterleave_heads(w_qk):
    d, n = w_qk.shape
    return w_qk.reshape(d, n // RET_KEY_DIM, RET_KEY_DIM // 2, 2).transpose(0, 1, 3, 2).reshape(d, n)


def kernel(x, a_w_in, a_gn_g, a_gn_b, a_w_out, b_w_q, b_w_out, w_kv, ffn_w_gate, ffn_w_up, ffn_conv_w,
           ffn_conv_b, ffn_w_down, ln_mix_g, ln_mix_b, ln_ffn_g, ln_ffn_b):
    bsz, seq, d_model = x.shape
    n_a = a_w_in.shape[0]
    depth = ffn_w_gate.shape[0]
    assert depth == DEPTH
    ret_heads = d_model // RET_KEY_DIM
    t = bsz * seq
    xs = x.reshape(t, d_model)
    cos, sin = _rotary_tables(seq)
    wg_bf, wu_bf, wd_bf = ffn_w_gate.astype(_BF16), ffn_w_up.astype(_BF16), ffn_w_down.astype(_BF16)

    for l in range(depth):
        if l < n_a:
            w_in = a_w_in[l]
            w_perm = jnp.concatenate([_deinterleave_heads(w_in[:, :2 * d_model]), w_in[:, 2 * d_model:]], axis=1)
            h = _project(xs, w_perm.astype(_BF16), cos, sin, seq=seq, rot_cols=2 * d_model, q_cols=d_model,
                         k_scale=RET_KEY_DIM ** -0.5)
            o = _retention(h, a_gn_g[l], a_gn_b[l], bsz=bsz, seq=seq, d_model=d_model, heads=ret_heads)
            xs = _out_ln(o, a_w_out[l].astype(_BF16), xs, ln_mix_g[l], ln_mix_b[l])
        else:
            jl = l - n_a
            w_q = b_w_q[jl] * ((SB_HEAD_DIM ** -0.5) * LOG2E)
            if l == n_a:
                w_cat = jnp.concatenate([w_q, w_kv], axis=1).astype(_BF16)
                qkv = _project(xs, w_cat, cos, sin, seq=seq, rot_cols=0, q_cols=0, k_scale=1.0)
            else:
                q_new = _project(xs, w_q.astype(_BF16), cos, sin, seq=seq, rot_cols=0, q_cols=0, k_scale=1.0)
                qkv = jnp.concatenate([q_new, qkv[:, d_model:]], axis=1)
            o = _stick_breaking(qkv, bsz=bsz, seq=seq, d_model=d_model)
            xs = _out_ln(o, b_w_out[jl].astype(_BF16), xs, ln_mix_g[l], ln_mix_b[l])
        xs = _conv_ffn_ln(xs, wg_bf, wu_bf, ffn_conv_w[l], ffn_conv_b[l], wd_bf, ln_ffn_g[l], ln_ffn_b[l],
                          layer=l, seq=seq)
    return xs.reshape(bsz, seq, d_model)
```
